```python
import math
import jax
import jax.numpy as jnp
from jax import lax
import numpy as np

D_MODEL = 1024
BATCH = 2
SEQ = 8192
DEPTH = 4
DEC_BATCH = 128
DEC_SEQ = 8
PAST_LEN = 2048
PAGE_SIZE = 128

ATT_HEAD_DIM = 64
ATT_HEADS_PER_GROUP = 4
DIL_GROUPS = ((128, 1), (512, 4), (2048, 16))
N_DIL = 3
N_ATT_HEADS = N_DIL * ATT_HEADS_PER_GROUP
ATT_WIDTH = N_ATT_HEADS * ATT_HEAD_DIM
ATT_OUT = ATT_HEADS_PER_GROUP * ATT_HEAD_DIM
ROPE_THETA = 10000.0
HG_HEADS = 6
HG_HEAD_DIM = 128
HG_WIDTH = HG_HEADS * HG_HEAD_DIM
HG_CHUNK = 64
SSM_HEADS = 16
SSM_HEAD_DIM = 64
SSM_INNER = SSM_HEADS * SSM_HEAD_DIM
SSM_GROUPS = 2
SSM_STATE = 128
SSM_CONV = 4
SSM_CHUNK = 128
CONV_DIM = SSM_INNER + 2 * SSM_GROUPS * SSM_STATE
N_BRANCH = 3
IN_WIDTH = 3 * ATT_WIDTH + 4 * HG_WIDTH + SSM_INNER + CONV_DIM + SSM_HEADS + N_BRANCH * D_MODEL
N_EXPERTS = 32
TOP_K = 4
D_FF = D_MODEL
SWIGLU_LIMIT = 7.0
SWIGLU_ALPHA = 1.702
MOE_BLOCK = 128
ALPHA_DN = (2 * DEPTH) ** 0.25
BETA_DN = (8 * DEPTH) ** -0.25
LN_EPS = 1e-5
RMS_EPS = 1e-6

kernel_name = 'hybrid_dilated_hgrn2_ssd_moe_decoder_step'

F32 = jnp.float32


def layer_norm(x, g, b):
    xf = x.astype(F32)
    mu = jnp.mean(xf, -1, keepdims=True)
    var = jnp.mean(jnp.square(xf - mu), -1, keepdims=True)
    return ((xf - mu) * lax.rsqrt(var + LN_EPS) * g + b).astype(x.dtype)


def rms_norm_groups(x, g, group):
    xf = x.astype(F32)
    shp = xf.shape
    xg = xf.reshape(shp[:-1] + (shp[-1] // group, group))
    xg = xg * lax.rsqrt(jnp.mean(jnp.square(xg), -1, keepdims=True) + RMS_EPS)
    return xg.reshape(shp) * g


def rotary(x, pos):
    half = x.shape[-1] // 2
    inv = ROPE_THETA ** (-jnp.arange(half, dtype=F32) / half)
    ang = pos.astype(F32)[:, None] * inv[None, :]
    cos = jnp.cos(ang)[None, :, None, :]
    sin = jnp.sin(ang)[None, :, None, :]
    xf = x.astype(F32)
    x1, x2 = xf[..., :half], xf[..., half:]
    return jnp.concatenate([x1 * cos - x2 * sin, x2 * cos + x1 * sin], -1).astype(x.dtype)


def softmax_stats(s):
    mx = jnp.max(s, -1, keepdims=True)
    p = jnp.exp(s - mx)
    den = jnp.sum(p, -1, keepdims=True)
    return p / den, (mx + jnp.log(den))[..., 0]


def dilated_attn_prompt(q, k, v, window, dil):
    B, S, H, d = q.shape
    n = window // dil
    s_pad = -(-S // window) * window
    m = s_pad // dil
    nb = m // n
    padw = ((0, 0), (0, s_pad - S), (0, 0), (0, 0))

    def blocks(t):
        t = jnp.pad(t, padw).reshape(B, m, dil, H, d).transpose(0, 2, 1, 3, 4)
        return t.reshape(B, dil, nb, n, H, d)

    def with_prev(t):
        prev = jnp.pad(t[:, :, :-1], ((0, 0), (0, 0), (1, 0), (0, 0), (0, 0), (0, 0)))
        return jnp.concatenate([prev, t], axis=3)

    qb = blocks(q)
    kk = with_prev(blocks(k))
    vv = with_prev(blocks(v))
    s = jnp.einsum('brnqhd,brnkhd->brnhqk', qb, kk, preferred_element_type=F32) * (d ** -0.5)
    a = jnp.arange(n)[:, None]
    j = jnp.arange(2 * n)[None, :]
    band = (j >= a) & (j <= a + n)
    has_prev = (jnp.arange(nb) > 0)[:, None, None]
    mask = band[None] & (has_prev | (j >= n)[None])
    s = jnp.where(mask[None, None, :, None], s, -jnp.inf)
    p, lse = softmax_stats(s)
    o = jnp.einsum('brnhqk,brnkhd->brnqhd', p, vv)
    o = o.reshape(B, dil, m, H, d).transpose(0, 2, 1, 3, 4).reshape(B, s_pad, H, d)[:, :S]
    lse = lse.transpose(0, 1, 2, 4, 3).reshape(B, dil, m, H).transpose(0, 2, 1, 3)
    lse = lse.reshape(B, s_pad, H)[:, :S]
    return o, lse


def dilated_attn_sample(q, k, v, buf, window, dil):
    B, T, H, d = q.shape
    L = buf.shape[1]
    n = window // dil
    kall = jnp.concatenate([buf[:, :, 0], k], axis=1)
    vall = jnp.concatenate([buf[:, :, 1], v], axis=1)
    idx = (L + jnp.arange(T))[:, None] - dil * jnp.arange(n + 1)[None, :]
    valid = idx >= 0
    idx = jnp.maximum(idx, 0)
    kg = kall[:, idx]
    vg = vall[:, idx]
    s = jnp.einsum('bthd,btkhd->bhtk', q, kg, preferred_element_type=F32) * (d ** -0.5)
    s = jnp.where(valid[None, None], s, -jnp.inf)
    p, lse = softmax_stats(s)
    o = jnp.einsum('bhtk,btkhd->bthd', p, vg)
    return o, lse.transpose(0, 2, 1)


def gla_chunked(q, k, v, logf, s0, chunk):
    B, T, H, dk = q.shape
    dv = v.shape[-1]
    nc = T // chunk

    def to_chunks(t):
        return t.reshape((B, nc, chunk) + t.shape[2:]).swapaxes(0, 1)

    causal = jnp.tril(jnp.ones((chunk, chunk), bool))

    def step(S, inp):
        qc, kc, vc, gc = inp
        b = jnp.cumsum(gc, axis=1)
        dec = jnp.exp(jnp.where(causal[None, :, :, None, None],
                                b[:, :, None] - b[:, None], -jnp.inf))
        att = jnp.einsum('bthd,btshd->bhts', qc, dec * kc[:, None])
        o = jnp.einsum('bhts,bshv->bthv', att, vc) + jnp.einsum('bthd,bhdv->bthv', qc * jnp.exp(b), S)
        bl = b[:, -1]
        S = S * jnp.exp(bl)[..., None] + jnp.einsum('bshd,bshv->bhdv', kc * jnp.exp(bl[:, None] - b), vc)
        return S, o

    S, o = lax.scan(step, s0, (to_chunks(q), to_chunks(k), to_chunks(v), to_chunks(logf)))
    return o.swapaxes(0, 1).reshape(B, T, H, dv), S


def ssd_chunked(x, dt, a, bm, cm, s0, chunk):
    B, T, H, P = x.shape
    G, N = bm.shape[2], bm.shape[3]
    hpg = H // G
    nc = T // chunk

    def to_chunks(t):
        return t.reshape((B, nc, chunk) + t.shape[2:]).swapaxes(0, 1)

    causal = jnp.tril(jnp.ones((chunk, chunk), bool))

    def step(s, inp):
        xc, dtc, bc, cc = inp
        cum = jnp.cumsum(dtc * a, axis=1)
        seg = jnp.exp(jnp.where(causal[None, :, :, None], cum[:, :, None] - cum[:, None], -jnp.inf))
        seg = seg.reshape(B, chunk, chunk, G, hpg)
        xdt = (xc * dtc[..., None]).reshape(B, chunk, G, hpg, P)
        cb = jnp.einsum('btgn,bsgn->btsg', cc, bc)
        y = jnp.einsum('btsg,btsgh,bsghp->btghp', cb, seg, xdt)
        sg = s.reshape(B, G, hpg, P, N)
        y = y + jnp.einsum('btgn,bghpn->btghp', cc, sg) * jnp.exp(cum).reshape(B, chunk, G, hpg)[..., None]
        w_end = jnp.exp(cum[:, -1:] - cum).reshape(B, chunk, G, hpg)
        s_new = sg * jnp.exp(cum[:, -1]).reshape(B, G, hpg)[..., None, None] \
            + jnp.einsum('bsgn,bsghp->bghpn', bc, xdt * w_end[..., None])
        return s_new.reshape(B, H, P, N), y.reshape(B, chunk, H, P)

    s, y = lax.scan(step, s0, (to_chunks(x), to_chunks(dt), to_chunks(bm), to_chunks(cm)))
    return y.swapaxes(0, 1).reshape(B, T, H, P), s


def hgrn2_branch(qr, fr, ir, gr, lb, norm_g, s0):
    B, T, _ = qr.shape
    shp = (B, T, HG_HEADS, HG_HEAD_DIM)
    q = jax.nn.silu(qr.astype(F32)).reshape(shp)
    f = lb + (1.0 - lb) * jax.nn.sigmoid(fr.astype(F32))
    logf = jnp.log(f).reshape(shp)
    k = (1.0 - f).reshape(shp)
    v = ir.astype(F32).reshape(shp)
    o, s = gla_chunked(q, k, v, logf, s0.astype(F32), math.gcd(T, HG_CHUNK))
    o = rms_norm_groups(o.reshape(B, T, HG_WIDTH), norm_g, HG_HEAD_DIM) * jax.nn.silu(gr.astype(F32))
    return o, s


def mamba2_branch(z, xbc, dt_raw, conv_w, conv_b, dt_bias, a_log, d_skip, norm_g, conv_buf, s0):
    B, T, _ = xbc.shape
    xpad = jnp.concatenate([conv_buf.astype(xbc.dtype), xbc], axis=1)
    new_buf = xpad[:, -(SSM_CONV - 1):]
    xc = lax.conv_general_dilated(xpad.astype(F32), conv_w.astype(F32)[:, None, :], (1,), 'VALID',
                                  dimension_numbers=('NWC', 'WIO', 'NWC'),
                                  feature_group_count=CONV_DIM) + conv_b
    xc = jax.nn.silu(xc)
    xs, bm, cm = jnp.split(xc, [SSM_INNER, SSM_INNER + SSM_GROUPS * SSM_STATE], axis=-1)
    xs = xs.reshape(B, T, SSM_HEADS, SSM_HEAD_DIM)
    dt = jax.nn.softplus(dt_raw.astype(F32) + dt_bias)
    a = -jnp.exp(a_log.astype(F32))
    y, s = ssd_chunked(xs, dt, a, bm.reshape(B, T, SSM_GROUPS, SSM_STATE),
                       cm.reshape(B, T, SSM_GROUPS, SSM_STATE), s0.astype(F32), math.gcd(T, SSM_CHUNK))
    y = y + d_skip[:, None] * xs
    y = y.reshape(B, T, SSM_INNER) * jax.nn.silu(z.astype(F32))
    y = rms_norm_groups(y, norm_g, SSM_INNER // SSM_GROUPS)
    return y, s, new_buf


def token_mixers(h, pos, lb, kv_bufs, hg0, ssm0, conv0, lp):
    B, T, _ = h.shape
    u = h @ lp['w_in']
    sizes = [ATT_WIDTH] * 3 + [HG_WIDTH] * 4 + [SSM_INNER, CONV_DIM, SSM_HEADS, N_BRANCH * D_MODEL]
    points = [int(v) for v in np.cumsum(sizes)[:-1]]
    qa, ka, va, qh, fh, ih, gh, z, xbc, dt_raw, gates = jnp.split(u, points, axis=-1)

    ahs = (B, T, N_ATT_HEADS, ATT_HEAD_DIM)
    qa = rotary(qa.reshape(ahs), pos)
    ka = rotary(ka.reshape(ahs), pos)
    va = va.reshape(ahs)
    outs, lses, rows = [], [], []
    for g, (win, dil) in enumerate(DIL_GROUPS):
        hs = slice(g * ATT_HEADS_PER_GROUP, (g + 1) * ATT_HEADS_PER_GROUP)
        qg, kg, vg = qa[:, :, hs], ka[:, :, hs], va[:, :, hs]
        if kv_bufs is None:
            o, lse = dilated_attn_prompt(qg, kg, vg, win, dil)
            rows.append(jnp.stack([kg, vg], axis=2)[:, -min(win, T):])
        else:
            o, lse = dilated_attn_sample(qg, kg, vg, kv_bufs[g], win, dil)
            rows.append(jnp.stack([kg, vg], axis=2))
        outs.append(o)
        lses.append(lse)
    wts = jax.nn.softmax(jnp.stack(lses), axis=0)
    ya = jnp.einsum('gbth,gbthd->bthd', wts, jnp.stack(outs)).reshape(B, T, ATT_OUT)

    yh, hg_s = hgrn2_branch(qh, fh, ih, gh, lb, lp['hg_norm_g'], hg0)

    ys, ssm_s, conv_s = mamba2_branch(z, xbc, dt_raw, lp['conv_w'], lp['conv_b'], lp['dt_bias'],
                                      lp['a_log'], lp['d_skip'], lp['ssm_norm_g'], conv0, ssm0)

    ga, gb, gc = jnp.split(jax.nn.sigmoid(gates.astype(F32)), N_BRANCH, axis=-1)
    merged = ga * (ya @ lp['w_br_att']) + gb * (yh @ lp['w_br_hg']) + gc * (ys @ lp['w_br_ssm'])
    return merged @ lp['w_out'], (rows[0], rows[1], rows[2], hg_s, ssm_s, conv_s)


def moe(h, w_router, b_router, w_gu, b_gu, w_dn, b_dn):
    shp = h.shape
    xt = h.reshape(-1, shp[-1])
    T = xt.shape[0]
    logits = (xt @ w_router + b_router).astype(F32)
    top_v, top_i = lax.top_k(logits, TOP_K)
    gate = jax.nn.softmax(top_v, axis=-1)
    tk = T * TOP_K
    e_flat = top_i.reshape(tk)
    tok_flat = jnp.repeat(jnp.arange(T, dtype=jnp.int32), TOP_K)
    g_flat = gate.reshape(tk)
    order = jnp.argsort(e_flat)
    e_s, tok_s, g_s = e_flat[order], tok_flat[order], g_flat[order]
    counts = jnp.zeros((N_EXPERTS,), jnp.int32).at[e_flat].add(1)
    starts = jnp.cumsum(counts) - counts
    padded = (counts + MOE_BLOCK - 1) // MOE_BLOCK * MOE_BLOCK
    pends = jnp.cumsum(padded)
    pstarts = pends - padded
    dest = pstarts[e_s] + jnp.arange(tk, dtype=jnp.int32) - starts[e_s]
    nb = -(-tk // MOE_BLOCK) + N_EXPERTS
    slot_tok = jnp.zeros((nb * MOE_BLOCK,), jnp.int32).at[dest].set(tok_s)
    slot_g = jnp.zeros((nb * MOE_BLOCK,), F32).at[dest].set(g_s)
    block_e = jnp.minimum(jnp.searchsorted(pends // MOE_BLOCK, jnp.arange(nb), side='right'),
                          N_EXPERTS - 1)

    def expert_block(args):
        tok, e = args
        xb = xt[tok]
        gu = xb @ w_gu[e] + b_gu[e]
        g, up = jnp.split(gu, 2, axis=-1)
        g = jnp.minimum(g, SWIGLU_LIMIT)
        up = jnp.clip(up, -SWIGLU_LIMIT, SWIGLU_LIMIT)
        hid = (up + 1.0) * g * jax.nn.sigmoid(SWIGLU_ALPHA * g)
        return (hid @ w_dn[e] + b_dn[e]).astype(F32)

    outs = lax.map(expert_block, (slot_tok.reshape(nb, MOE_BLOCK), block_e))
    y = jnp.zeros((T, shp[-1]), F32).at[slot_tok].add(outs.reshape(-1, shp[-1]) * slot_g[:, None])
    return y.astype(h.dtype).reshape(shp)


def trunk(x, c, pos, caches, p, lb_all):
    B = x.shape[0]
    new_states = []
    for l in range(DEPTH):
        lp = {name: arr[l] for name, arr in p.items()}
        if caches is None:
            kv_bufs = None
            hg0 = jnp.zeros((B, HG_HEADS, HG_HEAD_DIM, HG_HEAD_DIM), F32)
            ssm0 = jnp.zeros((B, SSM_HEADS, SSM_HEAD_DIM, SSM_STATE), F32)
            conv0 = jnp.zeros((B, SSM_CONV - 1, CONV_DIM), x.dtype)
        else:
            kv_bufs = (caches[0][l], caches[1][l], caches[2][l])
            hg0, ssm0, conv0 = caches[3][l], caches[4][l], caches[5][l]
        mod = jax.nn.silu(c.astype(F32)) @ lp['w_ada'] + lp['b_ada']
        sh1, sc1, g1, sh2, sc2, g2 = [m[:, None, :] for m in jnp.split(mod, 6, axis=-1)]
        h = x * (1.0 + sc1) + sh1
        mix, st = token_mixers(h, pos, lb_all[l], kv_bufs, hg0, ssm0, conv0, lp)
        x = layer_norm(ALPHA_DN * x + g1 * mix, lp['ln1_g'], lp['ln1_b'])
        h2 = x * (1.0 + sc2) + sh2
        ff = moe(h2, lp['w_router'], lp['b_router'], lp['w_gu'], lp['b_gu'], lp['w_dn'], lp['b_dn'])
        x = layer_norm(ALPHA_DN * x + g2 * ff, lp['ln2_g'], lp['ln2_b'])
        new_states.append(st)
    stacked = [jnp.stack([st[i] for st in new_states]) for i in range(6)]
    return x, stacked


def setup_inputs(seed: int = 0) -> dict:
    key = jax.random.key(seed)
    ks = iter(jax.random.split(key, 64))

    def nrm(shape, scale):
        return jax.random.normal(next(ks), shape, F32) * scale

    D = D_MODEL
    kv_shape = lambda w: (DEPTH, DEC_BATCH, min(w, PAST_LEN), 2, ATT_HEADS_PER_GROUP, ATT_HEAD_DIM)
    dt0 = jnp.exp(jax.random.uniform(next(ks), (DEPTH, SSM_HEADS), F32, math.log(1e-3), math.log(1e-1)))
    dt_bias = dt0 + jnp.log(-jnp.expm1(-dt0))
    a_log = jnp.log(jax.random.uniform(next(ks), (DEPTH, SSM_HEADS), F32, 1.0, 16.0))
    return {
        'x_prompt': nrm((BATCH, SEQ, D), 1.0),
        'x_sample': nrm((DEC_BATCH, DEC_SEQ, D), 1.0),
        'c_prompt': nrm((BATCH, D), 1.0),
        'c_sample': nrm((DEC_BATCH, D), 1.0),
        'cache_kv_w128': nrm(kv_shape(DIL_GROUPS[0][0]), 1.0),
        'cache_kv_w512': nrm(kv_shape(DIL_GROUPS[1][0]), 1.0),
        'cache_kv_w2048': nrm(kv_shape(DIL_GROUPS[2][0]), 1.0),
        'state_hgrn': nrm((DEPTH, DEC_BATCH, HG_HEADS, HG_HEAD_DIM, HG_HEAD_DIM), 1.0),
        'state_ssm': nrm((DEPTH, DEC_BATCH, SSM_HEADS, SSM_HEAD_DIM, SSM_STATE), 1.0),
        'state_conv': nrm((DEPTH, DEC_BATCH, SSM_CONV - 1, CONV_DIM), 1.0),
        'w_in': nrm((DEPTH, D, IN_WIDTH), D ** -0.5),
        'w_br_att': nrm((DEPTH, ATT_OUT, D), ATT_OUT ** -0.5),
        'w_br_hg': nrm((DEPTH, HG_WIDTH, D), HG_WIDTH ** -0.5),
        'w_br_ssm': nrm((DEPTH, SSM_INNER, D), SSM_INNER ** -0.5),
        'w_out': nrm((DEPTH, D, D), BETA_DN * D ** -0.5),
        'conv_w': nrm((DEPTH, SSM_CONV, CONV_DIM), SSM_CONV ** -0.5),
        'conv_b': nrm((DEPTH, CONV_DIM), 0.02),
        'dt_bias': dt_bias,
        'a_log': a_log,
        'd_skip': 1.0 + nrm((DEPTH, SSM_HEADS), 0.02),
        'ssm_norm_g': 1.0 + nrm((DEPTH, SSM_INNER), 0.02),
        'hg_lb': nrm((DEPTH, HG_WIDTH), 0.1),
        'hg_norm_g': 1.0 + nrm((DEPTH, HG_WIDTH), 0.02),
        'ln1_g': 1.0 + nrm((DEPTH, D), 0.02),
        'ln1_b': nrm((DEPTH, D), 0.02),
        'ln2_g': 1.0 + nrm((DEPTH, D), 0.02),
        'ln2_b': nrm((DEPTH, D), 0.02),
        'w_ada': nrm((DEPTH, D, 6 * D), D ** -0.5),
        'b_ada': nrm((DEPTH, 6 * D), 0.02),
        'w_router': nrm((DEPTH, D, N_EXPERTS), D ** -0.5),
        'b_router': nrm((DEPTH, N_EXPERTS), 0.01),
        'w_gu': nrm((DEPTH, N_EXPERTS, D, 2 * D_FF), D ** -0.5),
        'b_gu': nrm((DEPTH, N_EXPERTS, 2 * D_FF), 0.01),
        'w_dn': nrm((DEPTH, N_EXPERTS, D_FF, D), BETA_DN * D_FF ** -0.5),
        'b_dn': nrm((DEPTH, N_EXPERTS, D), 0.01),
    }


def reference(x_prompt, x_sample, c_prompt, c_sample, cache_kv_w128, cache_kv_w512, cache_kv_w2048,
              state_hgrn, state_ssm, state_conv, w_in, w_br_att, w_br_hg, w_br_ssm, w_out, conv_w,
              conv_b, dt_bias, a_log, d_skip, ssm_norm_g, hg_lb, hg_norm_g, ln1_g, ln1_b, ln2_g,
              ln2_b, w_ada, b_ada, w_router, b_router, w_gu, b_gu, w_dn, b_dn):
    p = dict(w_in=w_in, w_br_att=w_br_att, w_br_hg=w_br_hg, w_br_ssm=w_br_ssm, w_out=w_out,
             conv_w=conv_w, conv_b=conv_b, dt_bias=dt_bias, a_log=a_log, d_skip=d_skip,
             ssm_norm_g=ssm_norm_g, hg_norm_g=hg_norm_g, ln1_g=ln1_g, ln1_b=ln1_b, ln2_g=ln2_g,
             ln2_b=ln2_b, w_ada=w_ada, b_ada=b_ada, w_router=w_router, b_router=b_router,
             w_gu=w_gu, b_gu=b_gu, w_dn=w_dn, b_dn=b_dn)
    lb = jax.nn.softmax(hg_lb.astype(F32), axis=0)
    lb_all = jnp.cumsum(lb, axis=0) - lb[0]
    pos_p = jnp.arange(x_prompt.shape[1], dtype=jnp.int32)
    pos_s = PAST_LEN + jnp.arange(x_sample.shape[1], dtype=jnp.int32)
    y_prompt, st_p = trunk(x_prompt, c_prompt, pos_p, None, p, lb_all)
    kv128_p, kv512_p, kv2048_p, hgrn_p, ssm_p, conv_p = st_p
    caches = (cache_kv_w128, cache_kv_w512, cache_kv_w2048, state_hgrn, state_ssm, state_conv)
    y_sample, st_s = trunk(x_sample, c_sample, pos_s, caches, p, lb_all)
    kv128_s, kv512_s, kv2048_s, hgrn_s, ssm_s, conv_s = st_s
    return (y_prompt, y_sample, kv128_p, kv512_p, kv2048_p, hgrn_p, ssm_p, conv_p,
            kv128_s, kv512_s, kv2048_s, hgrn_s, ssm_s, conv_s)
```

```python
import functools
import math

import jax
import jax.numpy as jnp
from jax import lax
from jax.experimental import pallas as pl
from jax.experimental.pallas import tpu as pltpu

F32 = jnp.float32
BF16 = jnp.bfloat16
I32 = jnp.int32

D_MODEL = 1024
DEPTH = 4
PAST_LEN = 2048
ATT_HEAD_DIM = 64
ATT_HPG = 4
DIL_GROUPS = ((128, 1), (512, 4), (2048, 16))
ATT_WIDTH = 768
ATT_OUT = 256
ROPE_THETA = 10000.0
HG_HEADS = 6
HG_HEAD_DIM = 128
HG_WIDTH = 768
SSM_HEADS = 16
SSM_HEAD_DIM = 64
SSM_INNER = 1024
SSM_GROUPS = 2
SSM_STATE = 128
SSM_CONV = 4
CONV_DIM = 1536
N_EXPERTS = 32
TOP_K = 4
D_FF = 1024
SWIGLU_LIMIT = 7.0
SWIGLU_ALPHA = 1.702
ALPHA_DN = (2 * DEPTH) ** 0.25
LN_EPS = 1e-5
RMS_EPS = 1e-6
ATT_N = 128

COL_ATT = 0
COL_HG = 2304
COL_DT = 5376
COL_Z = 6144
COL_XBC = 7680
COL_GATE = 9216
NP_COLS = 12288

LANES = 128
SUBLANES = 8
VMEM_LIMIT = 56 * 1024 * 1024
MOE_BM = 256
HG_CHUNK = 16
SSD_CHUNK = 128


def _cp(sem, vmem=VMEM_LIMIT):
    return pltpu.CompilerParams(dimension_semantics=sem, vmem_limit_bytes=vmem)


def _silu(x):
    return x * jax.nn.sigmoid(x)


def _dot(a, b):
    return jnp.dot(a, b, preferred_element_type=F32)


def _dot_nt(a, b):
    return lax.dot_general(a, b, (((1,), (1,)), ((), ())), preferred_element_type=F32)


def _dot_tn(a, b):
    return lax.dot_general(a, b, (((0,), (0,)), ((), ())), preferred_element_type=F32)


def _split3(x):
    hi = x.astype(BF16)
    r1 = x - hi.astype(F32)
    mid = r1.astype(BF16)
    lo = (r1 - mid.astype(F32)).astype(BF16)
    return hi, mid, lo


def _dot01_left(m01, x):
    hi, mid, lo = _split3(x)
    return _dot(m01, hi) + _dot(m01, mid) + _dot(m01, lo)


def _dot01_right(x, m01):
    hi, mid, lo = _split3(x)
    return _dot(hi, m01) + _dot(mid, m01) + _dot(lo, m01)


def _dot_f32(a, b):
    ah, am, al = _split3(a)
    bh, bm, bl = _split3(b)
    return (_dot(ah, bh) + _dot(ah, bm) + _dot(am, bh)) + (_dot(ah, bl) + _dot(al, bh) + _dot(am, bm))


def _iota(shape, dim):
    return lax.broadcasted_iota(I32, shape, dim)


def _cumsum_seg(x, seg):
    pos = _iota((x.shape[0], 1), 0) % seg
    sh = 1
    while sh < seg:
        x = x + jnp.where(pos >= sh, pltpu.roll(x, sh, 0), 0.0)
        sh *= 2
    return x


def _pad_rows(x, rows):
    if x.shape[0] >= rows:
        return x
    return jnp.concatenate([x, jnp.zeros((rows - x.shape[0],) + x.shape[1:], x.dtype)], axis=0)


def _pick(n, options):
    for o in options:
        if n % o == 0:
            return o
    raise ValueError(f"no tile in {options} divides {n}")


def _ada_kernel(c_ref, w_ref, b_ref, o_ref):
    s = _silu(c_ref[...])
    o_ref[...] = _dot(s.astype(BF16), w_ref[...].astype(BF16)) + b_ref[...]


def _ada_call(c_groups, w_ada, b_ada):
    G, D = c_groups.shape
    depth = w_ada.shape[0]
    nj = w_ada.shape[2] // D
    return pl.pallas_call(
        _ada_kernel,
        grid=(depth, nj),
        in_specs=[pl.BlockSpec((G, D), lambda l, j: (0, 0)),
                  pl.BlockSpec((None, D, D), lambda l, j: (l, 0, j)),
                  pl.BlockSpec((None, 1, D), lambda l, j: (l, 0, j))],
        out_specs=pl.BlockSpec((None, G, D), lambda l, j: (l, 0, j)),
        out_shape=jax.ShapeDtypeStruct((depth, G, nj * D), F32),
        compiler_params=_cp(("parallel", "parallel")),
    )(c_groups, w_ada, b_ada.reshape(depth, 1, nj * D))


def _inproj_kernel(x_ref, sh_ref, sc_ref, w_ref, o_ref, h_scr):
    @pl.when(pl.program_id(1) == 0)
    def _():
        h = x_ref[...] * (1.0 + sc_ref[...]) + sh_ref[...]
        h_scr[...] = h.reshape(h_scr.shape).astype(BF16)

    o_ref[...] = _dot(h_scr[...], w_ref[...]).astype(o_ref.dtype)


def _inproj_call(x3, mod4, w_in_p, layer, tn=1024):
    G, _, D = x3.shape
    T = G * SUBLANES
    tm = _pick(T, (1024, 512, 256, 128))
    tg = tm // SUBLANES
    npc = w_in_p.shape[2]
    return pl.pallas_call(
        _inproj_kernel,
        grid=(T // tm, npc // tn),
        in_specs=[pl.BlockSpec((tg, SUBLANES, D), lambda i, j: (i, 0, 0)),
                  pl.BlockSpec((None, tg, 1, D), lambda i, j: (layer, i, 0, 0)),
                  pl.BlockSpec((None, tg, 1, D), lambda i, j: (layer, i, 0, 1)),
                  pl.BlockSpec((None, D, tn), lambda i, j: (layer, 0, j))],
        out_specs=pl.BlockSpec((tm, tn), lambda i, j: (i, j)),
        out_shape=jax.ShapeDtypeStruct((T, npc), BF16),
        scratch_shapes=[pltpu.VMEM((tm, D), BF16)],
        compiler_params=_cp(("parallel", "arbitrary")),
    )(x3, mod4, mod4, w_in_p)


def _rope(x, cos, sin_signed):
    outs = []
    for blk in range(x.shape[1] // LANES):
        xx = x[:, blk * LANES:(blk + 1) * LANES]
        lane = _iota(xx.shape, 1)
        first_half = (lane % ATT_HEAD_DIM) < (ATT_HEAD_DIM // 2)
        rot = jnp.where(first_half, pltpu.roll(xx, LANES - 32, 1), pltpu.roll(xx, 32, 1))
        outs.append(xx * cos[:, blk * LANES:(blk + 1) * LANES] + rot * sin_signed[:, blk * LANES:(blk + 1) * LANES])
    return jnp.concatenate(outs, axis=1)


def _rope_tables(pos, width):
    half = ATT_HEAD_DIM // 2
    inv = ROPE_THETA ** (-jnp.arange(half, dtype=F32) / half)
    ang = pos.astype(F32)[:, None] * inv[None, :]
    cos = jnp.cos(ang)
    sin = jnp.sin(ang)
    reps = width // ATT_HEAD_DIM
    return (jnp.tile(jnp.concatenate([cos, cos], -1), (1, reps)),
            jnp.tile(jnp.concatenate([-sin, sin], -1), (1, reps)))


def _head_masks(shape):
    lane = _iota(shape, 1)
    return [(lane // ATT_HEAD_DIM) == h for h in range(ATT_HPG)]


def _attn_p_kernel(q_ref, k_ref, v_ref, kp_ref, vp_ref, cos_ref, sin_ref, cosp_ref, sinp_ref,
                   o_ref, lse_ref, kr_ref, *, tq):
    n = ATT_N
    i = pl.program_id(2)
    cos = cos_ref[...]
    sin = sin_ref[...]
    q = (_rope(q_ref[...].astype(F32), cos, sin) * (ATT_HEAD_DIM ** -0.5)).astype(BF16)
    k = _rope(k_ref[...].astype(F32), cos, sin)
    kr_ref[...] = k
    kb = k.astype(BF16)
    kpb = _rope(kp_ref[...].astype(F32), cosp_ref[...], sinp_ref[...]).astype(BF16)
    v = v_ref[...]
    vp = vp_ref[...]
    masks = _head_masks((n, ATT_OUT))
    row = _iota((n, 2 * n), 0)
    col = _iota((n, 2 * n), 1)
    band = (col >= row) & (col <= row + n)
    first_thr = jnp.where(i > 0, 0, n)
    for c in range(tq // n):
        qc = q[c * n:(c + 1) * n]
        if c == 0:
            kk = jnp.concatenate([kpb, kb[0:n]], axis=0)
            vv = jnp.concatenate([vp, v[0:n]], axis=0)
            mask = band & (col >= first_thr)
        else:
            kk = kb[(c - 1) * n:(c + 1) * n]
            vv = v[(c - 1) * n:(c + 1) * n]
            mask = band
        o_acc = jnp.zeros((n, ATT_OUT), F32)
        l_acc = jnp.zeros((n, ATT_OUT), F32)
        for h in range(ATT_HPG):
            qh = jnp.where(masks[h], qc, jnp.zeros_like(qc))
            s = jnp.where(mask, _dot_nt(qh, kk), -jnp.inf)
            m = jnp.max(s, axis=-1, keepdims=True)
            p = jnp.exp(s - m)
            den = jnp.sum(p, axis=-1, keepdims=True)
            oh = _dot(p.astype(BF16), vv) / den
            o_acc = jnp.where(masks[h], oh, o_acc)
            l_acc = jnp.where(masks[h], m + jnp.log(den), l_acc)
        o_ref[c * n:(c + 1) * n, :] = o_acc
        lse_ref[c * n:(c + 1) * n, :] = l_acc


def _attn_p_call(u, cos_t, sin_t, g, dil, B, S, tq):
    T, npc = u.shape
    n = ATT_N
    sd = S // dil
    tq = min(tq, sd)
    nq = sd // tq
    cb = npc // ATT_OUT
    u2 = u.reshape(T // dil, dil * npc)
    cos2 = cos_t.reshape(sd, dil * ATT_OUT)
    sin2 = sin_t.reshape(sd, dil * ATT_OUT)
    rows = B * S

    def cur(off):
        return pl.BlockSpec((tq, ATT_OUT), lambda b, r, i: (b * nq + i, r * cb + off))

    def prev(off):
        return pl.BlockSpec((n, ATT_OUT),
                            lambda b, r, i: (jnp.maximum(b * (sd // n) + i * (tq // n) - 1, 0), r * cb + off))

    tab = pl.BlockSpec((tq, ATT_OUT), lambda b, r, i: (i, r))
    tabp = pl.BlockSpec((n, ATT_OUT), lambda b, r, i: (jnp.maximum(i * (tq // n) - 1, 0), r))
    out_spec = pl.BlockSpec((tq, ATT_OUT), lambda b, r, i: (b * nq + i, r))
    out_sh = jax.ShapeDtypeStruct((rows // dil, dil * ATT_OUT), F32)
    qo, ko, vo = (COL_ATT // ATT_OUT + g, (COL_ATT + ATT_WIDTH) // ATT_OUT + g, (COL_ATT + 2 * ATT_WIDTH) // ATT_OUT + g)
    o, lse, kr = pl.pallas_call(
        functools.partial(_attn_p_kernel, tq=tq),
        grid=(B, dil, nq),
        in_specs=[cur(qo), cur(ko), cur(vo), prev(ko), prev(vo), tab, tab, tabp, tabp],
        out_specs=[out_spec, out_spec, out_spec],
        out_shape=[out_sh, out_sh, out_sh],
        compiler_params=_cp(("parallel", "parallel", "arbitrary")),
    )(u2, u2, u2, u2, u2, cos2, sin2, cos2, sin2)
    return o.reshape(rows, ATT_OUT), lse.reshape(rows, ATT_OUT), kr.reshape(rows, ATT_OUT)


def _attn_s_kernel(q_ref, k_ref, v_ref, c0_ref, c1_ref, c2_ref, cos_ref, sin_ref,
                   ya_ref, kv0_ref, kv1_ref, kv2_ref, *, bb, dt):
    cos = cos_ref[...]
    sin = sin_ref[...]
    qa = _rope(q_ref[...].astype(F32), jnp.tile(cos, (bb, 1)), jnp.tile(sin, (bb, 1))) * (ATT_HEAD_DIM ** -0.5)
    ka = _rope(k_ref[...].astype(F32), jnp.tile(cos, (bb, 1)), jnp.tile(sin, (bb, 1)))
    va = v_ref[...].astype(F32)
    masks = _head_masks((dt, ATT_OUT))
    nr = ATT_HPG * dt
    kv_refs = (kv0_ref, kv1_ref, kv2_ref)
    cache_refs = (c0_ref, c1_ref, c2_ref)
    for bi in range(bb):
        rs = slice(bi * dt, (bi + 1) * dt)
        s_list, v_list = [], []
        for g, (win, dil) in enumerate(DIL_GROUPS):
            cs = slice(g * ATT_OUT, (g + 1) * ATT_OUT)
            qg = qa[rs, cs]
            kg = ka[rs, cs]
            vg = va[rs, cs]
            kv_refs[g][bi] = jnp.concatenate([kg, vg], axis=1)
            qexp = jnp.concatenate([jnp.where(masks[h], qg, 0.0) for h in range(ATT_HPG)], axis=0).astype(BF16)
            cache = cache_refs[g][bi]
            if cache.ndim == 3:
                cache = cache.reshape(cache.shape[0] * cache.shape[1], cache.shape[2])
            L = cache.shape[0]
            kc = cache[:, :ATT_OUT].astype(BF16)
            vc = cache[:, ATT_OUT:].astype(BF16)
            t_r = _iota((nr, L), 0) % dt
            c_i = _iota((nr, L), 1)
            if g == 2:
                ok = (c_i % dt) == t_r
            else:
                ok = ((c_i + dil * dt - t_r) % dil == 0) & (c_i >= t_r)
            s_list.append(jnp.where(ok, _dot_nt(qexp, kc), -jnp.inf))
            v_list.append(vc)
            t_n = _iota((nr, 2 * dt), 0) % dt
            c_n = _iota((nr, 2 * dt), 1)
            okn = (c_n <= t_n) & ((t_n - c_n + dil * dt) % dil == 0)
            s_list.append(jnp.where(okn, _dot_nt(qexp, _pad_rows(kg, 2 * dt).astype(BF16)), -jnp.inf))
            v_list.append(_pad_rows(vg, 2 * dt).astype(BF16))
        m = functools.reduce(jnp.maximum, [jnp.max(s, axis=-1, keepdims=True) for s in s_list])
        den = jnp.zeros((nr, 1), F32)
        o = jnp.zeros((nr, ATT_OUT), F32)
        for s, vv in zip(s_list, v_list):
            p = jnp.exp(s - m)
            den = den + jnp.sum(p, axis=-1, keepdims=True)
            o = o + _dot(p.astype(BF16), vv)
        o = o / den
        y = jnp.zeros((dt, ATT_OUT), F32)
        for h in range(ATT_HPG):
            y = jnp.where(masks[h], o[h * dt:(h + 1) * dt], y)
        ya_ref[rs, :] = y.astype(ya_ref.dtype)


def _attn_s_call(u, caches, cos_s, sin_s, layer, row0, DB, DT, bb=4):
    c0, c1, c2 = caches
    nb = DB // bb
    rb = bb * DT
    wb = ATT_WIDTH
    l2 = c2.shape[2]
    in_specs = [pl.BlockSpec((rb, wb), lambda i: (row0 // rb + i, 0)),
                pl.BlockSpec((rb, wb), lambda i: (row0 // rb + i, 1)),
                pl.BlockSpec((rb, wb), lambda i: (row0 // rb + i, 2)),
                pl.BlockSpec((None, bb, c0.shape[2], 512), lambda i: (layer, i, 0, 0)),
                pl.BlockSpec((None, bb, c1.shape[2], 512), lambda i: (layer, i, 0, 0)),
                pl.BlockSpec((None, bb, l2, DT, 512), lambda i: (layer, i, 0, 0, 0)),
                pl.BlockSpec((DT, wb), lambda i: (0, 0)),
                pl.BlockSpec((DT, wb), lambda i: (0, 0))]
    kv_spec = pl.BlockSpec((bb, DT, 512), lambda i: (i, 0, 0))
    kv_sh = jax.ShapeDtypeStruct((DB, DT, 512), F32)
    return pl.pallas_call(
        functools.partial(_attn_s_kernel, bb=bb, dt=DT),
        grid=(nb,),
        in_specs=in_specs,
        out_specs=[pl.BlockSpec((rb, ATT_OUT), lambda i: (i, 0)), kv_spec, kv_spec, kv_spec],
        out_shape=[jax.ShapeDtypeStruct((DB * DT, ATT_OUT), BF16), kv_sh, kv_sh, kv_sh],
        compiler_params=_cp(("parallel",)),
    )(u, u, u, c0, c1, c2, cos_s, sin_s)


def _hgrn_lb(lbp_ref, layer):
    p = lbp_ref[...]
    mx = jnp.max(p, axis=0, keepdims=True)
    e = jnp.exp(p - mx)
    sm = e / jnp.sum(e, axis=0, keepdims=True)
    lb = jnp.zeros((1, p.shape[1]), F32)
    for i in range(1, layer + 1):
        lb = lb + sm[i:i + 1]
    return lb


def _hgrn_gates(qr, fr, lb):
    q = _silu(qr)
    f = lb + (1.0 - lb) * jax.nn.sigmoid(fr)
    return q, jnp.log(f), 1.0 - f


def _hgrn_chunk_head(q, k, v, b, st):
    C = q.shape[0]
    mxu_rows = 2 * SUBLANES
    o_inter = _dot_nt(_pad_rows(q * jnp.exp(b), mxu_rows).astype(BF16), st.astype(BF16))[:C]
    r8 = _iota((SUBLANES, 1), 0)
    rows = []
    for tb in range(C // SUBLANES):
        qt = q[tb * SUBLANES:(tb + 1) * SUBLANES]
        bt = b[tb * SUBLANES:(tb + 1) * SUBLANES]
        acc = jnp.zeros((SUBLANES, q.shape[1]), F32)
        for s in range((tb + 1) * SUBLANES):
            e = jnp.exp(jnp.minimum(bt - b[s:s + 1], 0.0))
            a = jnp.sum(qt * k[s:s + 1] * e, axis=1, keepdims=True)
            if s >= tb * SUBLANES:
                a = jnp.where(r8 >= (s - tb * SUBLANES), a, 0.0)
            acc = acc + a * v[s:s + 1]
        rows.append(acc)
    o = o_inter + (jnp.concatenate(rows, axis=0) if len(rows) > 1 else rows[0])
    bl = b[C - 1:C]
    ke = k * jnp.exp(bl - b)
    upd = _dot_tn(_pad_rows(v, mxu_rows).astype(BF16), _pad_rows(ke, mxu_rows).astype(BF16))
    return o, st * jnp.exp(bl) + upd


def _hgrn_rows(qr, fr, ir, gr, lb, ng, st_get, st_set, C):
    q, logf, k = _hgrn_gates(qr, fr, lb)
    b = _cumsum_seg(logf, C)
    outs = []
    for h in range(HG_HEADS):
        sl = slice(h * HG_HEAD_DIM, (h + 1) * HG_HEAD_DIM)
        o, st_new = _hgrn_chunk_head(q[:, sl], k[:, sl], ir[:, sl], b[:, sl], st_get(h))
        st_set(h, st_new)
        outs.append(o * lax.rsqrt(jnp.mean(o * o, axis=-1, keepdims=True) + RMS_EPS))
    return jnp.concatenate(outs, axis=1) * ng * _silu(gr)


def _hgrn_p_kernel(q_ref, f_ref, i_ref, g_ref, lbp_ref, ng_ref, y_ref, st_ref, st_scr, *, tt, C, layer):
    t = pl.program_id(1)

    @pl.when(t == 0)
    def _():
        st_scr[...] = jnp.zeros_like(st_scr)

    lb = _hgrn_lb(lbp_ref, layer)
    ng = ng_ref[...]

    def chunk(c, carry):
        r0 = pl.multiple_of(c * C, C)
        rs = pl.ds(r0, C)

        def st_set(h, val):
            st_scr[h] = val

        y = _hgrn_rows(q_ref[rs, :].astype(F32), f_ref[rs, :].astype(F32), i_ref[rs, :].astype(F32),
                       g_ref[rs, :].astype(F32), lb, ng, lambda h: st_scr[h], st_set, C)
        y_ref[rs, :] = y.astype(y_ref.dtype)
        return carry

    lax.fori_loop(0, tt // C, chunk, 0)

    @pl.when(t == pl.num_programs(1) - 1)
    def _():
        for h in range(HG_HEADS):
            st_ref[h] = st_scr[h].T


def _hgrn_p_call(u, hg_lb, hg_norm_g, layer, B, S, tt=256):
    nt = S // tt
    wb = HG_WIDTH
    c0 = COL_HG // wb

    def col(o):
        return pl.BlockSpec((tt, wb), lambda b, t: (b * nt + t, c0 + o))

    return pl.pallas_call(
        functools.partial(_hgrn_p_kernel, tt=tt, C=HG_CHUNK, layer=layer),
        grid=(B, nt),
        in_specs=[col(0), col(1), col(2), col(3),
                  pl.BlockSpec(hg_lb.shape, lambda b, t: (0, 0)),
                  pl.BlockSpec((None, 1, wb), lambda b, t: (layer, 0, 0))],
        out_specs=[pl.BlockSpec((tt, wb), lambda b, t: (b * nt + t, 0)),
                   pl.BlockSpec((None, HG_HEADS, HG_HEAD_DIM, HG_HEAD_DIM), lambda b, t: (b, 0, 0, 0))],
        out_shape=[jax.ShapeDtypeStruct((B * S, wb), BF16),
                   jax.ShapeDtypeStruct((B, HG_HEADS, HG_HEAD_DIM, HG_HEAD_DIM), F32)],
        scratch_shapes=[pltpu.VMEM((HG_HEADS, HG_HEAD_DIM, HG_HEAD_DIM), F32)],
        compiler_params=_cp(("parallel", "arbitrary")),
    )(u, u, u, u, hg_lb, hg_norm_g.reshape(DEPTH, 1, wb))


def _hgrn_s_kernel(q_ref, f_ref, i_ref, g_ref, s0_ref, lbp_ref, ng_ref, y_ref, s1_ref,
                   qs, fs, is_, gs, ys, *, bb, dt, layer):
    lb = _hgrn_lb(lbp_ref, layer)
    ng = ng_ref[...]
    qs[...] = q_ref[...].astype(F32)
    fs[...] = f_ref[...].astype(F32)
    is_[...] = i_ref[...].astype(F32)
    gs[...] = g_ref[...].astype(F32)

    def seq(j, carry):
        rs = pl.ds(pl.multiple_of(j * dt, dt), dt)

        def st_set(h, val):
            s1_ref[j, h] = val.T

        ys[rs, :] = _hgrn_rows(qs[rs, :], fs[rs, :], is_[rs, :], gs[rs, :], lb, ng,
                               lambda h: s0_ref[j, h].T, st_set, dt)
        return carry

    lax.fori_loop(0, bb, seq, 0)
    y_ref[...] = ys[...].astype(y_ref.dtype)


def _hgrn_s_call(u, state, hg_lb, hg_norm_g, layer, row0, DB, DT, bb=8):
    wb = HG_WIDTH
    rb = bb * DT
    c0 = COL_HG // wb

    def col(o):
        return pl.BlockSpec((rb, wb), lambda i: (row0 // rb + i, c0 + o))

    st_spec = pl.BlockSpec((None, bb, HG_HEADS, HG_HEAD_DIM, HG_HEAD_DIM), lambda i: (layer, i, 0, 0, 0))
    return pl.pallas_call(
        functools.partial(_hgrn_s_kernel, bb=bb, dt=DT, layer=layer),
        grid=(DB // bb,),
        in_specs=[col(0), col(1), col(2), col(3), st_spec,
                  pl.BlockSpec(hg_lb.shape, lambda i: (0, 0)),
                  pl.BlockSpec((None, 1, wb), lambda i: (layer, 0, 0))],
        out_specs=[pl.BlockSpec((rb, wb), lambda i: (i, 0)),
                   pl.BlockSpec((bb, HG_HEADS, HG_HEAD_DIM, HG_HEAD_DIM), lambda i: (i, 0, 0, 0))],
        out_shape=[jax.ShapeDtypeStruct((DB * DT, wb), BF16),
                   jax.ShapeDtypeStruct((DB, HG_HEADS, HG_HEAD_DIM, HG_HEAD_DIM), F32)],
        scratch_shapes=[pltpu.VMEM((rb, wb), F32)] * 5,
        compiler_params=_cp(("parallel",)),
    )(u, u, u, u, state, hg_lb, hg_norm_g.reshape(DEPTH, 1, wb))


def _softplus(x):
    return jnp.maximum(x, 0.0) + jnp.log1p(jnp.exp(-jnp.abs(x)))


def _ssd_expand_mats():
    h_i = _iota((LANES, SSM_INNER), 0)
    e64 = (h_i == _iota((LANES, SSM_INNER), 1) // SSM_HEAD_DIM).astype(BF16)
    h_j = _iota((LANES, SSM_HEADS * LANES), 0)
    e128 = (h_j == _iota((LANES, SSM_HEADS * LANES), 1) // LANES).astype(BF16)
    return e64, e128


def _ssd_common(xc, z, dtr, prm, seq_len):
    C = xc.shape[0]
    dtb, alog, dsk, ng = prm
    xc = _silu(xc)
    xs = xc[:, :SSM_INNER]
    bm = xc[:, SSM_INNER:SSM_INNER + SSM_GROUPS * SSM_STATE]
    cm = xc[:, SSM_INNER + SSM_GROUPS * SSM_STATE:]
    dt = _softplus(dtr + dtb)
    dta = dt * (-jnp.exp(alog))
    r_i = _iota((C, C), 0)
    c_i = _iota((C, C), 1)
    same = (r_i // seq_len) == (c_i // seq_len)
    causal = same & (r_i >= c_i)
    cum = _dot01_left(causal.astype(BF16), dta)
    e64, e128 = _ssd_expand_mats()
    cum_e = _dot01_right(cum, e64)
    dt_e = _dot01_right(dt, e64)
    cum_c = _dot01_right(cum, e128)
    cum_t = cum.T
    xdt = xs * dt_e
    lane = _iota((C, LANES), 1)
    ys = []
    cbs = [_dot_nt(cm[:, g * SSM_STATE:(g + 1) * SSM_STATE].astype(BF16),
                   bm[:, g * SSM_STATE:(g + 1) * SSM_STATE].astype(BF16)) for g in range(SSM_GROUPS)]
    hpg = SSM_HEADS // SSM_GROUPS
    for hp in range(SSM_HEADS // 2):
        ms = []
        for h in (2 * hp, 2 * hp + 1):
            dec = jnp.exp(jnp.minimum(cum_c[:, h * LANES:h * LANES + C] - cum_t[h:h + 1, :], 0.0))
            ms.append(jnp.where(causal, cbs[h // hpg] * dec, 0.0))
        mcat = jnp.concatenate(ms, axis=1).astype(BF16)
        xp = xdt[:, hp * LANES:(hp + 1) * LANES]
        xbd = jnp.concatenate([jnp.where(lane < SSM_HEAD_DIM, xp, 0.0),
                               jnp.where(lane >= SSM_HEAD_DIM, xp, 0.0)], axis=0).astype(BF16)
        ys.append(_dot(mcat, xbd))
    y_intra = jnp.concatenate(ys, axis=1)
    return dict(xs=xs, bm=bm, cm=cm, cum=cum, cum_e=cum_e, cum_t=cum_t, xdt=xdt, y_intra=y_intra,
                z=z, dsk=dsk, ng=ng)


def _ssd_finish(y, it):
    y = y + it["dsk"] * it["xs"]
    y = y * _silu(it["z"])
    half = SSM_INNER // SSM_GROUPS
    outs = []
    for g in range(SSM_GROUPS):
        yg = y[:, g * half:(g + 1) * half]
        outs.append(yg * lax.rsqrt(jnp.mean(yg * yg, axis=-1, keepdims=True) + RMS_EPS))
    return jnp.concatenate(outs, axis=1) * it["ng"]


def _ssd_p_kernel(z_ref, xbc_ref, dt_ref, cw_ref, cb_ref, dtb_ref, alog_ref, dsk_ref, ng_ref,
                  y_ref, st_ref, st_scr, prev_scr, *, C):
    t = pl.program_id(1)

    @pl.when(t == 0)
    def _():
        st_scr[...] = jnp.zeros_like(st_scr)
        prev_scr[...] = jnp.zeros_like(prev_scr)

    x = xbc_ref[...].astype(F32)
    prev8 = prev_scr[...]
    cw = cw_ref[...]
    r8 = _iota((SUBLANES, 1), 0)
    xc = cb_ref[...] + cw[SSM_CONV - 1:SSM_CONV] * x
    for k in range(1, SSM_CONV):
        sh = pltpu.roll(x, k, 0)
        head = jnp.where(r8 < k, pltpu.roll(prev8, k, 0), sh[:SUBLANES])
        xc = xc + cw[SSM_CONV - 1 - k:SSM_CONV - k] * jnp.concatenate([head, sh[SUBLANES:]], axis=0)
    prev_scr[...] = x[C - SUBLANES:]
    it = _ssd_common(xc, z_ref[...].astype(F32), dt_ref[...].astype(F32)[:, :LANES],
                     (dtb_ref[...], alog_ref[...], dsk_ref[...], ng_ref[...]), C)
    half = SSM_INNER // SSM_GROUPS
    st = st_scr[...]
    cl_e = it["cum_e"][C - 1:C]
    y_inter = jnp.concatenate(
        [_dot(it["cm"][:, g * SSM_STATE:(g + 1) * SSM_STATE].astype(BF16), st[:, g * half:(g + 1) * half].astype(BF16))
         for g in range(SSM_GROUPS)], axis=1) * jnp.exp(it["cum_e"])
    w = (it["xdt"] * jnp.exp(cl_e - it["cum_e"])).astype(BF16)
    upd = jnp.concatenate(
        [_dot_tn(it["bm"][:, g * SSM_STATE:(g + 1) * SSM_STATE].astype(BF16), w[:, g * half:(g + 1) * half])
         for g in range(SSM_GROUPS)], axis=1)
    st_scr[...] = st * jnp.exp(cl_e) + upd
    y_ref[...] = _ssd_finish(it["y_intra"] + y_inter, it).astype(y_ref.dtype)

    @pl.when(t == pl.num_programs(1) - 1)
    def _():
        for j in range(SSM_INNER // LANES):
            st_ref[j * LANES:(j + 1) * LANES, :] = st_scr[:, j * LANES:(j + 1) * LANES].T


def _ssd_params(p, layer):
    def padl(v):
        return jnp.pad(v[layer], (0, LANES - SSM_HEADS)).reshape(1, LANES)

    return (p["conv_w"][layer], p["conv_b"][layer].reshape(1, CONV_DIM), padl(p["dt_bias"]), padl(p["a_log"]),
            jnp.repeat(p["d_skip"][layer], SSM_HEAD_DIM).reshape(1, SSM_INNER),
            p["ssm_norm_g"][layer].reshape(1, SSM_INNER))


def _full(a):
    nd = a.ndim
    return pl.BlockSpec(a.shape, lambda *_: (0,) * nd)


def _ssd_p_call(u, prm, B, S):
    C = SSD_CHUNK
    nt = S // C
    in_specs = [pl.BlockSpec((C, SSM_INNER), lambda b, t: (b * nt + t, COL_Z // SSM_INNER)),
                pl.BlockSpec((C, CONV_DIM), lambda b, t: (b * nt + t, COL_XBC // CONV_DIM)),
                pl.BlockSpec((C, HG_WIDTH), lambda b, t: (b * nt + t, COL_DT // HG_WIDTH))] + [_full(a) for a in prm]
    return pl.pallas_call(
        functools.partial(_ssd_p_kernel, C=C),
        grid=(B, nt),
        in_specs=in_specs,
        out_specs=[pl.BlockSpec((C, SSM_INNER), lambda b, t: (b * nt + t, 0)),
                   pl.BlockSpec((None, SSM_INNER, SSM_STATE), lambda b, t: (b, 0, 0))],
        out_shape=[jax.ShapeDtypeStruct((B * S, SSM_INNER), BF16),
                   jax.ShapeDtypeStruct((B, SSM_INNER, SSM_STATE), F32)],
        scratch_shapes=[pltpu.VMEM((SSM_STATE, SSM_INNER), F32), pltpu.VMEM((SUBLANES, CONV_DIM), F32)],
        compiler_params=_cp(("parallel", "arbitrary")),
    )(u, u, u, *prm)


def _ssd_s_kernel(z_ref, xbc_ref, dt_ref, c0_ref, s0_ref, cw_ref, cb_ref, dtb_ref, alog_ref, dsk_ref, ng_ref,
                  y_ref, s1_ref, *, bb, dt):
    C = bb * dt
    x = xbc_ref[...].astype(F32)
    c0 = c0_ref[...]
    c0 = jnp.concatenate([c0, jnp.zeros((bb, dt - c0.shape[1], CONV_DIM), F32)], axis=1).reshape(C, CONV_DIM)
    cw = cw_ref[...]
    tpos = _iota((C, 1), 0) % dt
    xc = cb_ref[...] + cw[SSM_CONV - 1:SSM_CONV] * x
    for k in range(1, SSM_CONV):
        term = jnp.where(tpos < k, pltpu.roll(c0, C + k - (SSM_CONV - 1), 0), pltpu.roll(x, k, 0))
        xc = xc + cw[SSM_CONV - 1 - k:SSM_CONV - k] * term
    it = _ssd_common(xc, z_ref[...].astype(F32), dt_ref[...].astype(F32)[:, :LANES],
                     (dtb_ref[...], alog_ref[...], dsk_ref[...], ng_ref[...]), dt)
    half = SSM_INNER // SSM_GROUPS
    e64t = (_iota((SSM_INNER, LANES), 0) // SSM_HEAD_DIM == _iota((SSM_INNER, LANES), 1)).astype(BF16)
    cum_et = _dot01_left(e64t, it["cum_t"])
    row_seq = _iota((C, 1), 0) // dt
    lane_seq = _iota((1, C), 1) // dt
    cl_col = jnp.concatenate([jnp.broadcast_to(it["cum_e"][(j + 1) * dt - 1:(j + 1) * dt], (dt, SSM_INNER))
                              for j in range(bb)], axis=0)
    w_t = (it["xdt"] * jnp.exp(cl_col - it["cum_e"])).T
    y_inter = jnp.zeros((C, SSM_INNER), F32)
    for j in range(bb):
        s_j = s0_ref[j]
        yj = jnp.concatenate(
            [_dot_nt(it["cm"][:, g * SSM_STATE:(g + 1) * SSM_STATE].astype(BF16),
                     s_j[g * half:(g + 1) * half].astype(BF16)) for g in range(SSM_GROUPS)], axis=1)
        y_inter = jnp.where(row_seq == j, yj, y_inter)
        wj = jnp.where(lane_seq == j, w_t, 0.0).astype(BF16)
        upd = jnp.concatenate(
            [_dot(wj[g * half:(g + 1) * half], it["bm"][:, g * SSM_STATE:(g + 1) * SSM_STATE].astype(BF16))
             for g in range(SSM_GROUPS)], axis=0)
        dec = jnp.exp(cum_et[:, (j + 1) * dt - 1:(j + 1) * dt])
        s1_ref[j] = s_j * dec + upd
    y_ref[...] = _ssd_finish(it["y_intra"] + y_inter * jnp.exp(it["cum_e"]), it).astype(y_ref.dtype)


def _ssd_s_call(u, conv_state, ssm_state, prm, layer, row0, DB, DT, bb=16):
    rb = bb * DT
    in_specs = [pl.BlockSpec((rb, SSM_INNER), lambda i: (row0 // rb + i, COL_Z // SSM_INNER)),
                pl.BlockSpec((rb, CONV_DIM), lambda i: (row0 // rb + i, COL_XBC // CONV_DIM)),
                pl.BlockSpec((rb, HG_WIDTH), lambda i: (row0 // rb + i, COL_DT // HG_WIDTH)),
                pl.BlockSpec((None, bb, SSM_CONV - 1, CONV_DIM), lambda i: (layer, i, 0, 0)),
                pl.BlockSpec((None, bb, SSM_INNER, SSM_STATE), lambda i: (layer, i, 0, 0))] + [_full(a) for a in prm]
    return pl.pallas_call(
        functools.partial(_ssd_s_kernel, bb=bb, dt=DT),
        grid=(DB // bb,),
        in_specs=in_specs,
        out_specs=[pl.BlockSpec((rb, SSM_INNER), lambda i: (i, 0)),
                   pl.BlockSpec((bb, SSM_INNER, SSM_STATE), lambda i: (i, 0, 0))],
        out_shape=[jax.ShapeDtypeStruct((DB * DT, SSM_INNER), BF16),
                   jax.ShapeDtypeStruct((DB, SSM_INNER, SSM_STATE), F32)],
        compiler_params=_cp(("parallel",)),
    )(u, u, u, conv_state, ssm_state, *prm)


def _layer_norm(v, g, b):
    mu = jnp.mean(v, axis=-1, keepdims=True)
    d = v - mu
    var = jnp.mean(d * d, axis=-1, keepdims=True)
    return d * lax.rsqrt(var + LN_EPS) * g + b


def _merge_kernel(x_ref, gate_ref, o0, o1, o2, l0, l1, l2, yas_ref, yhp_ref, yhs_ref, ysp_ref, yss_ref,
                  g1_ref, sh2_ref, sc2_ref, wa_ref, wh_ref, ws_ref, wo_ref, lng_ref, lnb_ref, wr_ref, br_ref,
                  x1_ref, h2_ref, ri_ref, rg_ref, cnt_ref, carry, *, n_prompt_tiles, tm):
    i = pl.program_id(0)

    @pl.when(i == 0)
    def _():
        carry[...] = jnp.zeros_like(carry)

    is_p = i < n_prompt_tiles
    ls = [l0[...], l1[...], l2[...]]
    mx = jnp.maximum(jnp.maximum(ls[0], ls[1]), ls[2])
    es = [jnp.exp(l - mx) for l in ls]
    ya_p = (es[0] * o0[...] + es[1] * o1[...] + es[2] * o2[...]) / (es[0] + es[1] + es[2])
    ya = jnp.where(is_p, ya_p.astype(BF16), yas_ref[...])
    yh = jnp.where(is_p, yhp_ref[...], yhs_ref[...])
    ys = jnp.where(is_p, ysp_ref[...], yss_ref[...])
    gts = jax.nn.sigmoid(gate_ref[...].astype(F32))
    m = (gts[:, :D_MODEL] * _dot(ya, wa_ref[...]) + gts[:, D_MODEL:2 * D_MODEL] * _dot(yh, wh_ref[...])
         + gts[:, 2 * D_MODEL:] * _dot(ys, ws_ref[...]))
    mix = _dot(m.astype(BF16), wo_ref[...])
    tg = tm // SUBLANES
    x = x_ref[...]
    v = ALPHA_DN * x + g1_ref[...] * mix.reshape(tg, SUBLANES, D_MODEL)
    x1 = _layer_norm(v, lng_ref[...], lnb_ref[...])
    x1_ref[...] = x1
    h2 = (x1 * (1.0 + sc2_ref[...]) + sh2_ref[...]).reshape(tm, D_MODEL)
    h2_ref[...] = h2
    logits = _dot_f32(h2, wr_ref[...]) + br_ref[...]
    lane = _iota(logits.shape, 1).astype(F32)
    cur = logits
    vals, idxs = [], []
    for _ in range(TOP_K):
        mval = jnp.max(cur, axis=-1, keepdims=True)
        idx = jnp.min(jnp.where(cur == mval, lane, float(N_EXPERTS)), axis=-1, keepdims=True)
        vals.append(mval)
        idxs.append(idx)
        cur = jnp.where(lane == idx, -jnp.inf, cur)
    ex = [jnp.exp(vv - vals[0]) for vv in vals]
    den = ex[0] + ex[1] + ex[2] + ex[3]
    onehot = jnp.zeros(logits.shape, F32)
    for idx in idxs:
        onehot = onehot + (lane == idx).astype(F32)
    strict = (_iota((tm, tm), 0) > _iota((tm, tm), 1)).astype(BF16)
    prefix = _dot(strict, onehot.astype(BF16)) + carry[...]
    carry[...] = carry[...] + jnp.sum(onehot, axis=0, keepdims=True)
    cnt_ref[...] = carry[...]
    l128 = _iota((tm, LANES), 1)
    ri = jnp.zeros((tm, LANES), I32)
    rg = jnp.zeros((tm, LANES), F32)
    for k in range(TOP_K):
        rank = jnp.sum(jnp.where(lane == idxs[k], prefix, 0.0), axis=-1, keepdims=True).astype(I32)
        ri = jnp.where(l128 == k, idxs[k].astype(I32), ri)
        ri = jnp.where(l128 == TOP_K + k, rank, ri)
        rg = jnp.where(l128 == k, ex[k] / den, rg)
    ri_ref[...] = ri
    rg_ref[...] = rg


def _merge_call(x3, u, att_p, ya_s, yh_p, yh_s, ys_p, ys_s, mod4, wts, layer, n_prompt_rows, tm=256):
    G, _, D = x3.shape
    T = G * SUBLANES
    tg = tm // SUBLANES
    npt = n_prompt_rows // tm
    o0, l0, o1, l1, o2, l2 = att_p
    wa, wh, ws, wo, lng, lnb, wr, br = wts

    def prow(w):
        return pl.BlockSpec((tm, w), lambda i: (jnp.minimum(i, npt - 1), 0))

    def srow(w):
        return pl.BlockSpec((tm, w), lambda i: (jnp.maximum(i - npt, 0), 0))

    def modspec(c):
        return pl.BlockSpec((None, tg, 1, D), lambda i: (layer, i, 0, c))

    in_specs = [pl.BlockSpec((tg, SUBLANES, D), lambda i: (i, 0, 0)),
                pl.BlockSpec((tm, 3 * D), lambda i: (i, COL_GATE // (3 * D))),
                prow(ATT_OUT), prow(ATT_OUT), prow(ATT_OUT), prow(ATT_OUT), prow(ATT_OUT), prow(ATT_OUT),
                srow(ATT_OUT), prow(HG_WIDTH), srow(HG_WIDTH), prow(SSM_INNER), srow(SSM_INNER),
                modspec(2), modspec(3), modspec(4)] + [_full(a) for a in wts]
    return pl.pallas_call(
        functools.partial(_merge_kernel, n_prompt_tiles=npt, tm=tm),
        grid=(T // tm,),
        in_specs=in_specs,
        out_specs=[pl.BlockSpec((tg, SUBLANES, D), lambda i: (i, 0, 0)),
                   pl.BlockSpec((tm, D), lambda i: (i, 0)),
                   pl.BlockSpec((tm, LANES), lambda i: (i, 0)),
                   pl.BlockSpec((tm, LANES), lambda i: (i, 0)),
                   pl.BlockSpec((1, N_EXPERTS), lambda i: (0, 0))],
        out_shape=[jax.ShapeDtypeStruct((G, SUBLANES, D), F32),
                   jax.ShapeDtypeStruct((T, D), F32),
                   jax.ShapeDtypeStruct((T, LANES), I32),
                   jax.ShapeDtypeStruct((T, LANES), F32),
                   jax.ShapeDtypeStruct((1, N_EXPERTS), F32)],
        scratch_shapes=[pltpu.VMEM((1, N_EXPERTS), F32)],
        compiler_params=_cp(("arbitrary",)),
    )(x3, u, o0, o1, o2, l0, l1, l2, ya_s, yh_p, yh_s, ys_p, ys_s, mod4, mod4, mod4, *wts)


def _dispatch_kernel(dest_ref, h_ref, xs_in_ref, xs_ref, sem, *, td):
    del xs_in_ref

    def row_copy(t, k):
        return pltpu.make_async_copy(h_ref.at[pl.ds(t, 1)], xs_ref.at[pl.ds(dest_ref[t * TOP_K + k], 1)], sem)

    def body(t, carry):
        for k in range(TOP_K):
            row_copy(t, k).start()
        return carry

    lax.fori_loop(0, td, body, 0)

    def drain(t, carry):
        for k in range(TOP_K):
            row_copy(t, k).wait()
        return carry

    lax.fori_loop(0, td, drain, 0)


def _dispatch_call(dest_flat, h2, xs_zero, td=256):
    T, D = h2.shape
    return pl.pallas_call(
        functools.partial(_dispatch_kernel, td=td),
        grid=(T // td,),
        in_specs=[pl.BlockSpec((td * TOP_K,), lambda i: (i,), memory_space=pltpu.SMEM),
                  pl.BlockSpec((td, D), lambda i: (i, 0)),
                  pl.BlockSpec(memory_space=pl.ANY)],
        out_specs=pl.BlockSpec(memory_space=pl.ANY),
        out_shape=jax.ShapeDtypeStruct(xs_zero.shape, xs_zero.dtype),
        scratch_shapes=[pltpu.SemaphoreType.DMA(())],
        input_output_aliases={2: 0},
        compiler_params=_cp(("arbitrary",)),
    )(dest_flat, h2, xs_zero)


def _expert_kernel(be_ref, nv_ref, x_ref, wgu_ref, bgu_ref, wdn_ref, bdn_ref, o_ref, wgu_bf, wdn_bf):
    j = pl.program_id(0)
    changed = jnp.logical_or(j == 0, be_ref[j] != be_ref[jnp.maximum(j - 1, 0)])

    @pl.when(changed)
    def _():
        wgu_bf[...] = wgu_ref[...].astype(BF16)
        wdn_bf[...] = wdn_ref[...].astype(BF16)

    @pl.when(j < nv_ref[0])
    def _():
        gu = _dot(x_ref[...].astype(BF16), wgu_bf[...]) + bgu_ref[...]
        g = jnp.minimum(gu[:, :D_FF], SWIGLU_LIMIT)
        up = jnp.clip(gu[:, D_FF:], -SWIGLU_LIMIT, SWIGLU_LIMIT)
        hid = (up + 1.0) * g * jax.nn.sigmoid(SWIGLU_ALPHA * g)
        o_ref[...] = _dot(hid.astype(BF16), wdn_bf[...]) + bdn_ref[...]

    @pl.when(j >= nv_ref[0])
    def _():
        o_ref[...] = jnp.zeros_like(o_ref)


def _expert_call(block_e, nvalid, xs, w_gu, b_gu4, w_dn, b_dn4, layer):
    NS, D = xs.shape
    bm = MOE_BM
    grid_spec = pltpu.PrefetchScalarGridSpec(
        num_scalar_prefetch=2,
        grid=(NS // bm,),
        in_specs=[pl.BlockSpec((bm, D), lambda j, be, nv: (jnp.minimum(j, nv[0] - 1), 0)),
                  pl.BlockSpec((None, None, D, 2 * D_FF), lambda j, be, nv: (layer, be[j], 0, 0)),
                  pl.BlockSpec((None, None, 1, 2 * D_FF), lambda j, be, nv: (layer, be[j], 0, 0)),
                  pl.BlockSpec((None, None, D_FF, D), lambda j, be, nv: (layer, be[j], 0, 0)),
                  pl.BlockSpec((None, None, 1, D), lambda j, be, nv: (layer, be[j], 0, 0))],
        out_specs=pl.BlockSpec((bm, D), lambda j, be, nv: (j, 0)),
        scratch_shapes=[pltpu.VMEM((D, 2 * D_FF), BF16), pltpu.VMEM((D_FF, D), BF16)])
    return pl.pallas_call(
        _expert_kernel,
        grid_spec=grid_spec,
        out_shape=jax.ShapeDtypeStruct((NS, D), F32),
        compiler_params=_cp(("arbitrary",)),
    )(block_e, nvalid, xs, w_gu, b_gu4, w_dn, b_dn4)


def _combine_kernel(dest_ref, rg_ref, x1_ref, g2_ref, lng_ref, lnb_ref, outs_ref, x2_ref, buf, sem, *, tc):
    def row_copy(t, k):
        return pltpu.make_async_copy(outs_ref.at[pl.ds(dest_ref[t * TOP_K + k], 1)], buf.at[k, pl.ds(t, 1)], sem)

    def body(t, carry):
        for k in range(TOP_K):
            row_copy(t, k).start()
        return carry

    lax.fori_loop(0, tc, body, 0)

    def drain(t, carry):
        for k in range(TOP_K):
            row_copy(t, k).wait()
        return carry

    lax.fori_loop(0, tc, drain, 0)
    rg = rg_ref[...]
    ff = rg[:, 0:1] * buf[0]
    for k in range(1, TOP_K):
        ff = ff + rg[:, k:k + 1] * buf[k]
    tg = tc // SUBLANES
    v = ALPHA_DN * x1_ref[...] + g2_ref[...] * ff.reshape(tg, SUBLANES, D_MODEL)
    x2_ref[...] = _layer_norm(v, lng_ref[...], lnb_ref[...])


def _combine_call(dest_flat, rg, x1_3, mod4, lng, lnb, outs, layer, tc=256):
    G, _, D = x1_3.shape
    T = G * SUBLANES
    tg = tc // SUBLANES
    return pl.pallas_call(
        functools.partial(_combine_kernel, tc=tc),
        grid=(T // tc,),
        in_specs=[pl.BlockSpec((tc * TOP_K,), lambda i: (i,), memory_space=pltpu.SMEM),
                  pl.BlockSpec((tc, LANES), lambda i: (i, 0)),
                  pl.BlockSpec((tg, SUBLANES, D), lambda i: (i, 0, 0)),
                  pl.BlockSpec((None, tg, 1, D), lambda i: (layer, i, 0, 5)),
                  _full(lng), _full(lnb),
                  pl.BlockSpec(memory_space=pl.ANY)],
        out_specs=pl.BlockSpec((tg, SUBLANES, D), lambda i: (i, 0, 0)),
        out_shape=jax.ShapeDtypeStruct((G, SUBLANES, D), F32),
        scratch_shapes=[pltpu.VMEM((TOP_K, tc, D), F32), pltpu.SemaphoreType.DMA(())],
        compiler_params=_cp(("arbitrary",)),
    )(dest_flat, rg, x1_3, mod4, lng, lnb, outs)


def _moe_plan(ri, counts):
    bm = MOE_BM
    T = ri.shape[0]
    cnt = counts.reshape(N_EXPERTS).astype(I32)
    padded = (cnt + bm - 1) // bm * bm
    pends = jnp.cumsum(padded)
    pstarts = pends - padded
    eidx = ri[:, :TOP_K]
    rank = ri[:, TOP_K:2 * TOP_K]
    dest = (pstarts[eidx] + rank).reshape(T * TOP_K)
    nb = (T * TOP_K) // bm + N_EXPERTS
    block_e = jnp.minimum(jnp.searchsorted(pends // bm, jnp.arange(nb, dtype=I32), side="right"),
                          N_EXPERTS - 1).astype(I32)
    nvalid = (pends[-1] // bm).astype(I32).reshape(1)
    last_e = block_e[jnp.maximum(nvalid[0] - 1, 0)]
    block_e = jnp.where(jnp.arange(nb) < nvalid[0], block_e, last_e)
    return dest, block_e, nvalid, nb * bm


def _pack_w_in(w_in):
    depth, D, _ = w_in.shape
    sizes = [ATT_WIDTH] * 3 + [HG_WIDTH] * 4 + [SSM_INNER, CONV_DIM, SSM_HEADS, 3 * D_MODEL]
    pts = [0]
    for s in sizes:
        pts.append(pts[-1] + s)
    att_hg = w_in[:, :, :pts[7]]
    z = w_in[:, :, pts[7]:pts[8]]
    xbc = w_in[:, :, pts[8]:pts[9]]
    dt = w_in[:, :, pts[9]:pts[10]]
    gates = w_in[:, :, pts[10]:pts[11]]

    def zeros(n):
        return jnp.zeros((depth, D, n), w_in.dtype)

    packed = jnp.concatenate([att_hg, dt, zeros(COL_Z - COL_DT - SSM_HEADS), z, zeros(COL_XBC - COL_Z - SSM_INNER),
                              xbc, gates], axis=2)
    assert packed.shape[2] == NP_COLS
    return packed.astype(BF16)


def kernel(x_prompt, x_sample, c_prompt, c_sample, cache_kv_w128, cache_kv_w512, cache_kv_w2048, state_hgrn, state_ssm, state_conv, w_in, w_br_att, w_br_hg, w_br_ssm, w_out, conv_w, conv_b, dt_bias, a_log, d_skip, ssm_norm_g, hg_lb, hg_norm_g, ln1_g, ln1_b, ln2_g, ln2_b, w_ada, b_ada, w_router, b_router, w_gu, b_gu, w_dn, b_dn):
    B, S, D = x_prompt.shape
    DB, DT, _ = x_sample.shape
    depth = w_in.shape[0]
    TP = B * S
    TS = DB * DT
    T = TP + TS
    G = T // SUBLANES
    p = dict(conv_w=conv_w, conv_b=conv_b, dt_bias=dt_bias, a_log=a_log, d_skip=d_skip, ssm_norm_g=ssm_norm_g)

    x3 = jnp.concatenate([x_prompt.reshape(TP, D), x_sample.reshape(TS, D)], axis=0).reshape(G, SUBLANES, D)
    c_groups = jnp.concatenate([jnp.repeat(c_prompt, S // SUBLANES, axis=0),
                                jnp.repeat(c_sample, DT // SUBLANES, axis=0)], axis=0)
    mod4 = _ada_call(c_groups, w_ada, b_ada).reshape(depth, G, 1, 6 * D)

    w_in_p = _pack_w_in(w_in)
    cos_p, sin_p = _rope_tables(jnp.arange(S, dtype=I32), ATT_OUT)
    cos_s, sin_s = _rope_tables(PAST_LEN + jnp.arange(DT, dtype=I32), ATT_WIDTH)
    kv_views = (cache_kv_w128.reshape(depth, DB, -1, 512), cache_kv_w512.reshape(depth, DB, -1, 512),
                cache_kv_w2048.reshape(depth, DB, -1, 16, 512))
    ssm_state_v = state_ssm.reshape(depth, DB, SSM_INNER, SSM_STATE)
    b_gu4 = b_gu.reshape(depth, N_EXPERTS, 1, 2 * D_FF)
    b_dn4 = b_dn.reshape(depth, N_EXPERTS, 1, D)

    outs_p = [[] for _ in range(6)]
    outs_s = [[] for _ in range(6)]
    for l in range(depth):
        u = _inproj_call(x3, mod4, w_in_p, l)

        att_p = []
        for g, (win, dil) in enumerate(DIL_GROUPS):
            o, lse, kr = _attn_p_call(u, cos_p, sin_p, g, dil, B, S, tq=512)
            att_p += [o, lse]
            keep = min(win, S)
            vcols = slice(COL_ATT + 2 * ATT_WIDTH + g * ATT_OUT, COL_ATT + 2 * ATT_WIDTH + (g + 1) * ATT_OUT)
            k_last = kr.reshape(B, S, ATT_HPG, ATT_HEAD_DIM)[:, S - keep:]
            v_last = u[:TP].reshape(B, S, -1)[:, S - keep:, vcols].astype(F32).reshape(B, keep, ATT_HPG, ATT_HEAD_DIM)
            outs_p[g].append(jnp.stack([k_last, v_last], axis=2))
        ya_s, kv0, kv1, kv2 = _attn_s_call(u, kv_views, cos_s, sin_s, l, TP, DB, DT)
        for g, kv in enumerate((kv0, kv1, kv2)):
            outs_s[g].append(kv.reshape(DB, DT, 2, ATT_HPG, ATT_HEAD_DIM))

        yh_p, hg_st_p = _hgrn_p_call(u, hg_lb, hg_norm_g, l, B, S)
        yh_s, hg_st_s = _hgrn_s_call(u, state_hgrn, hg_lb, hg_norm_g, l, TP, DB, DT)
        outs_p[3].append(hg_st_p)
        outs_s[3].append(hg_st_s)

        prm = _ssd_params(p, l)
        ys_p, ssm_st_p = _ssd_p_call(u, prm, B, S)
        ys_s, ssm_st_s = _ssd_s_call(u, state_conv, ssm_state_v, prm, l, TP, DB, DT)
        outs_p[4].append(ssm_st_p.reshape(B, SSM_HEADS, SSM_HEAD_DIM, SSM_STATE))
        outs_s[4].append(ssm_st_s.reshape(DB, SSM_HEADS, SSM_HEAD_DIM, SSM_STATE))
        xbc_cols = slice(COL_XBC, COL_XBC + CONV_DIM)
        outs_p[5].append(u[:TP].reshape(B, S, -1)[:, S - (SSM_CONV - 1):, xbc_cols].astype(F32))
        outs_s[5].append(u[TP:].reshape(DB, DT, -1)[:, DT - (SSM_CONV - 1):, xbc_cols].astype(F32))

        wts = (w_br_att[l].astype(BF16), w_br_hg[l].astype(BF16), w_br_ssm[l].astype(BF16), w_out[l].astype(BF16),
               ln1_g[l].reshape(1, D), ln1_b[l].reshape(1, D), w_router[l], b_router[l].reshape(1, N_EXPERTS))
        x1_3, h2, ri, rg, counts = _merge_call(x3, u, att_p, ya_s, yh_p, yh_s, ys_p, ys_s, mod4, wts, l, TP)

        dest, block_e, nvalid, ns = _moe_plan(ri, counts)
        xs = _dispatch_call(dest, h2, jnp.zeros((ns, D), F32))
        eo = _expert_call(block_e, nvalid, xs, w_gu, b_gu4, w_dn, b_dn4, l)
        x3 = _combine_call(dest, rg, x1_3, mod4, ln2_g[l].reshape(1, D), ln2_b[l].reshape(1, D), eo, l)

    xf = x3.reshape(T, D)
    y_prompt = xf[:TP].reshape(B, S, D)
    y_sample = xf[TP:].reshape(DB, DT, D)
    st_p = [jnp.stack(o) for o in outs_p]
    st_s = [jnp.stack(o) for o in outs_s]
    return (y_prompt, y_sample, *st_p, *st_s)
```

```python
import functools
import math

import jax
import jax.numpy as jnp
from jax import lax
from jax.experimental import pallas as pl
from jax.experimental.pallas import tpu as pltpu

F32 = jnp.float32
BF16 = jnp.bfloat16
I32 = jnp.int32

D_MODEL = 1024
DEPTH = 4
PAST_LEN = 2048
ATT_HEAD_DIM = 64
ATT_HPG = 4
DIL_GROUPS = ((128, 1), (512, 4), (2048, 16))
ATT_WIDTH = 768
ATT_OUT = 256
ROPE_THETA = 10000.0
HG_HEADS = 6
HG_HEAD_DIM = 128
HG_WIDTH = 768
SSM_HEADS = 16
SSM_HEAD_DIM = 64
SSM_INNER = 1024
SSM_GROUPS = 2
SSM_STATE = 128
SSM_CONV = 4
CONV_DIM = 1536
N_EXPERTS = 32
TOP_K = 4
D_FF = 1024
SWIGLU_LIMIT = 7.0
SWIGLU_ALPHA = 1.702
ALPHA_DN = (2 * DEPTH) ** 0.25
LN_EPS = 1e-5
RMS_EPS = 1e-6
ATT_N = 128

COL_ATT = 0
COL_HG = 2304
COL_DT = 5376
COL_Z = 6144
COL_XBC = 7680
COL_GATE = 9216
NP_COLS = 12288

LANES = 128
SUBLANES = 8
VMEM_LIMIT = 56 * 1024 * 1024
MOE_BM = 256
ATT_TILE = ATT_N * DIL_GROUPS[2][1]
HG_CHUNK = 16
SSD_CHUNK = 128


def _cp(sem, vmem=VMEM_LIMIT):
    return pltpu.CompilerParams(dimension_semantics=sem, vmem_limit_bytes=vmem)


def _silu(x):
    return x * jax.nn.sigmoid(x)


def _dot(a, b):
    return jnp.dot(a, b, preferred_element_type=F32)


def _dot_nt(a, b):
    return lax.dot_general(a, b, (((1,), (1,)), ((), ())), preferred_element_type=F32)


def _dot_tn(a, b):
    return lax.dot_general(a, b, (((0,), (0,)), ((), ())), preferred_element_type=F32)


def _split3(x):
    hi = x.astype(BF16)
    r1 = x - hi.astype(F32)
    mid = r1.astype(BF16)
    lo = (r1 - mid.astype(F32)).astype(BF16)
    return hi, mid, lo


def _dot01_left(m01, x):
    hi, mid, lo = _split3(x)
    return _dot(m01, hi) + _dot(m01, mid) + _dot(m01, lo)


def _dot01_right(x, m01):
    hi, mid, lo = _split3(x)
    return _dot(hi, m01) + _dot(mid, m01) + _dot(lo, m01)


def _dot_f32(a, b):
    ah, am, al = _split3(a)
    bh, bm, bl = _split3(b)
    return (_dot(ah, bh) + _dot(ah, bm) + _dot(am, bh)) + (_dot(ah, bl) + _dot(al, bh) + _dot(am, bm))


def _iota(shape, dim):
    return lax.broadcasted_iota(I32, shape, dim)


def _cumsum_seg(x, seg):
    pos = _iota((x.shape[0], 1), 0) % seg
    sh = 1
    while sh < seg:
        x = x + jnp.where(pos >= sh, pltpu.roll(x, sh, 0), 0.0)
        sh *= 2
    return x


def _pad_rows(x, rows):
    if x.shape[0] >= rows:
        return x
    return jnp.concatenate([x, jnp.zeros((rows - x.shape[0],) + x.shape[1:], x.dtype)], axis=0)


def _pick(n, options):
    for o in options:
        if n % o == 0:
            return o
    raise ValueError(f"no tile in {options} divides {n}")


def _ada_kernel(c_ref, w_ref, b_ref, o_ref):
    s = _silu(c_ref[...])
    o_ref[...] = _dot(s.astype(BF16), w_ref[...].astype(BF16)) + b_ref[...]


def _ada_call(c_groups, w_ada, b_ada):
    G, D = c_groups.shape
    depth = w_ada.shape[0]
    nj = w_ada.shape[2] // D
    return pl.pallas_call(
        _ada_kernel,
        grid=(depth, nj),
        in_specs=[pl.BlockSpec((G, D), lambda l, j: (0, 0)),
                  pl.BlockSpec((None, D, D), lambda l, j: (l, 0, j)),
                  pl.BlockSpec((None, 1, D), lambda l, j: (l, 0, j))],
        out_specs=pl.BlockSpec((None, G, D), lambda l, j: (l, 0, j)),
        out_shape=jax.ShapeDtypeStruct((depth, G, nj * D), F32),
        compiler_params=_cp(("parallel", "parallel")),
    )(c_groups, w_ada, b_ada.reshape(depth, 1, nj * D))


def _inproj_kernel(x_ref, sh_ref, sc_ref, w_ref, o_ref, h_scr):
    @pl.when(pl.program_id(1) == 0)
    def _():
        h = x_ref[...] * (1.0 + sc_ref[...]) + sh_ref[...]
        h_scr[...] = h.reshape(h_scr.shape).astype(BF16)

    o_ref[...] = _dot(h_scr[...], w_ref[...]).astype(o_ref.dtype)


def _inproj_s_call(x3, mod4, w_in_p, layer, group0, n_groups, tn=ATT_WIDTH):
    _, _, D = x3.shape
    tg = _pick(n_groups, (128, 64, 32, 16))
    tm = tg * SUBLANES
    npc = w_in_p.shape[2]
    g0 = group0 // tg
    return pl.pallas_call(
        _inproj_kernel,
        grid=(n_groups // tg, npc // tn),
        in_specs=[pl.BlockSpec((tg, SUBLANES, D), lambda i, j: (g0 + i, 0, 0)),
                  pl.BlockSpec((None, tg, 1, D), lambda i, j: (layer, g0 + i, 0, 0)),
                  pl.BlockSpec((None, tg, 1, D), lambda i, j: (layer, g0 + i, 0, 1)),
                  pl.BlockSpec((None, D, tn), lambda i, j: (layer, 0, j))],
        out_specs=pl.BlockSpec((tm, tn), lambda i, j: (i, j)),
        out_shape=jax.ShapeDtypeStruct((n_groups * SUBLANES, npc), BF16),
        scratch_shapes=[pltpu.VMEM((tm, D), BF16)],
        compiler_params=_cp(("parallel", "arbitrary")),
    )(x3, mod4, mod4, w_in_p)


def _inproj_p_kernel(x_ref, sh_ref, sc_ref, w_ref, o_ref, h_scr, xcol, *, tm):
    j = pl.program_id(1)
    ncol = xcol.shape[0]

    @pl.when(j == 0)
    def _():
        sc1 = 1.0 + sc_ref[...]
        sh = sh_ref[...]
        h_scr[0] = (x_ref[...] * sc1 + sh).astype(BF16)
        for c in range(ncol):
            xcol[c] = x_ref[:, c * LANES:(c + 1) * LANES]
        for gi in (1, 2):
            d = DIL_GROUPS[gi][1]
            n = tm // d
            for r in range(d):
                xr = jnp.concatenate([xcol[c, pl.ds(r, n, stride=d), :] for c in range(ncol)], axis=1)
                h_scr[gi, r * n:(r + 1) * n, :] = (xr * sc1 + sh).astype(BF16)

    sel = jnp.where(j == 1, 1, jnp.where(j == 2, 2, 0))
    o_ref[...] = _dot(h_scr[sel], w_ref[...]).astype(o_ref.dtype)


def _inproj_p_call(x2, mod4, w_in_p, layer, B, S, tm=ATT_TILE, tn=ATT_WIDTH):
    TP, D = x2.shape[0], x2.shape[1]
    TP = B * S
    npc = w_in_p.shape[2]
    tiles_per_seq = S // tm
    gps = S // SUBLANES

    def modspec(c):
        return pl.BlockSpec((None, None, 1, D), lambda i, j: (layer, (i // tiles_per_seq) * gps, 0, c))

    return pl.pallas_call(
        functools.partial(_inproj_p_kernel, tm=tm),
        grid=(TP // tm, npc // tn),
        in_specs=[pl.BlockSpec((tm, D), lambda i, j: (i, 0)), modspec(0), modspec(1),
                  pl.BlockSpec((None, D, tn), lambda i, j: (layer, 0, j))],
        out_specs=pl.BlockSpec((tm, tn), lambda i, j: (i, j)),
        out_shape=jax.ShapeDtypeStruct((TP, npc), BF16),
        scratch_shapes=[pltpu.VMEM((3, tm, D), BF16), pltpu.VMEM((D // LANES, tm, LANES), F32)],
        compiler_params=_cp(("parallel", "arbitrary")),
    )(x2, mod4, mod4, w_in_p)


def _rope(x, cos, sin_signed):
    outs = []
    for blk in range(x.shape[1] // LANES):
        xx = x[:, blk * LANES:(blk + 1) * LANES]
        lane = _iota(xx.shape, 1)
        first_half = (lane % ATT_HEAD_DIM) < (ATT_HEAD_DIM // 2)
        rot = jnp.where(first_half, pltpu.roll(xx, LANES - 32, 1), pltpu.roll(xx, 32, 1))
        outs.append(xx * cos + rot * sin_signed)
    return jnp.concatenate(outs, axis=1)


def _rope_tables(pos):
    half = ATT_HEAD_DIM // 2
    inv = ROPE_THETA ** (-jnp.arange(half, dtype=F32) / half)
    ang = pos.astype(F32)[:, None] * inv[None, :]
    cos = jnp.cos(ang)
    sin = jnp.sin(ang)
    reps = LANES // ATT_HEAD_DIM
    return (jnp.tile(jnp.concatenate([cos, cos], -1), (1, reps)),
            jnp.tile(jnp.concatenate([-sin, sin], -1), (1, reps)))


def _head_masks(shape):
    lane = _iota(shape, 1)
    return [(lane // ATT_HEAD_DIM) == h for h in range(ATT_HPG)]


def _attn_block(qc, kk, vv, mask, masks):
    o_acc = jnp.zeros(qc.shape, F32)
    l_acc = jnp.zeros(qc.shape, F32)
    for h in range(ATT_HPG):
        qh = jnp.where(masks[h], qc, jnp.zeros_like(qc))
        s = jnp.where(mask, _dot_nt(qh, kk), -jnp.inf)
        m = jnp.max(s, axis=-1, keepdims=True)
        p = jnp.exp(s - m)
        den = jnp.sum(p, axis=-1, keepdims=True)
        oh = _dot(p.astype(BF16), vv) / den
        o_acc = jnp.where(masks[h], oh, o_acc)
        l_acc = jnp.where(masks[h], m + jnp.log(den), l_acc)
    return o_acc, l_acc


def _st_slabs(ref, rows, val):
    for c in range(ref.shape[0]):
        ref[c, rows, :] = val[:, c * LANES:(c + 1) * LANES]


def _ld_slabs(ref, rows):
    return jnp.concatenate([ref[c, rows, :] for c in range(ref.shape[0])], axis=1)


def _lse_merge(o1, l1, o2, l2):
    m = jnp.maximum(l1, l2)
    e1 = jnp.exp(l1 - m)
    e2 = jnp.exp(l2 - m)
    den = e1 + e2
    return (e1 * o1 + e2 * o2) / den, m + jnp.log(den)


def _attn_p_kernel(u_ref, cos_ref, sin_ref, ya_ref, kv0_ref, kv1_ref, kv2_ref,
                   k0, v0, k1, v1, k2, v2, o_scr, l_scr, *, tb):
    n = ATT_N
    i = pl.program_id(1)
    scale = ATT_HEAD_DIM ** -0.5
    d1, d2 = DIL_GROUPS[1][1], DIL_GROUPS[2][1]
    n1, n2 = tb // d1, tb // d2

    @pl.when(i == 0)
    def _():
        k0[0:n, :] = jnp.zeros((n, ATT_OUT), BF16)
        v0[0:n, :] = jnp.zeros((n, ATT_OUT), BF16)
        k1[:, 0:n, :] = jnp.zeros((d1, n, ATT_OUT), BF16)
        v1[:, 0:n, :] = jnp.zeros((d1, n, ATT_OUT), BF16)
        k2[:, 0:n, :] = jnp.zeros((d2, n, ATT_OUT), BF16)
        v2[:, 0:n, :] = jnp.zeros((d2, n, ATT_OUT), BF16)

    masks = _head_masks((n, ATT_OUT))
    row = _iota((n, 2 * n), 0)
    col = _iota((n, 2 * n), 1)
    band = (col >= row) & (col <= row + n)
    first_mask = band & (col >= jnp.where(i > 0, 0, n))

    def qkv(rows, g, cos, sin):
        c0 = g * ATT_WIDTH
        q = (_rope(u_ref[rows, c0:c0 + ATT_OUT].astype(F32), cos, sin) * scale).astype(BF16)
        k = _rope(u_ref[rows, c0 + ATT_OUT:c0 + 2 * ATT_OUT].astype(F32), cos, sin)
        v = u_ref[rows, c0 + 2 * ATT_OUT:c0 + 3 * ATT_OUT]
        return q, k, v

    def stream2(r, carry):
        tok = pl.ds(r, n2, stride=d2)
        q, k, v = qkv(pl.ds(pl.multiple_of(r * n2, n2), n2), 2, cos_ref[tok, :], sin_ref[tok, :])
        k2[r, n:2 * n, :] = k.astype(BF16)
        v2[r, n:2 * n, :] = v
        _st_slabs(kv2_ref, tok, jnp.concatenate([k, v.astype(F32)], axis=1))
        o, l = _attn_block(q, k2[r], v2[r], first_mask, masks)
        _st_slabs(o_scr, tok, o)
        _st_slabs(l_scr, tok, l)
        k2[r, 0:n, :] = k2[r, n:2 * n, :]
        v2[r, 0:n, :] = v2[r, n:2 * n, :]
        return carry

    lax.fori_loop(0, d2, stream2, 0)

    def stream1(r, carry):
        tok_all = pl.ds(r, n1, stride=d1)
        q, k, v = qkv(pl.ds(pl.multiple_of(r * n1, n1), n1), 1, cos_ref[tok_all, :], sin_ref[tok_all, :])
        k1[r, n:n + n1, :] = k.astype(BF16)
        v1[r, n:n + n1, :] = v
        last = pl.ds(r, n, stride=d1)
        _st_slabs(kv1_ref, last, jnp.concatenate([k[n1 - n:], v[n1 - n:].astype(F32)], axis=1))
        for c in range(n1 // n):
            o, l = _attn_block(q[c * n:(c + 1) * n], k1[r, c * n:(c + 2) * n, :], v1[r, c * n:(c + 2) * n, :],
                               first_mask if c == 0 else band, masks)
            tok = pl.ds(r + c * n * d1, n, stride=d1)
            om, lm = _lse_merge(o, l, _ld_slabs(o_scr, tok), _ld_slabs(l_scr, tok))
            _st_slabs(o_scr, tok, om)
            _st_slabs(l_scr, tok, lm)
        k1[r, 0:n, :] = k1[r, n1:n1 + n, :]
        v1[r, 0:n, :] = v1[r, n1:n1 + n, :]
        return carry

    lax.fori_loop(0, d1, stream1, 0)

    nblk = tb // n

    def block0(c, carry):
        rows = pl.ds(pl.multiple_of(c * n, n), n)
        q, k, v = qkv(rows, 0, cos_ref[rows, :], sin_ref[rows, :])
        cur = pl.ds(pl.multiple_of(n + c * n, n), n)
        k0[cur, :] = k.astype(BF16)
        v0[cur, :] = v
        both = pl.ds(pl.multiple_of(c * n, n), 2 * n)
        mask = band & (col >= jnp.where(jnp.logical_or(c > 0, i > 0), 0, n))
        o, l = _attn_block(q, k0[both, :], v0[both, :], mask, masks)
        om, _ = _lse_merge(o, l, _ld_slabs(o_scr, rows), _ld_slabs(l_scr, rows))
        ya_ref[rows, :] = om.astype(ya_ref.dtype)

        @pl.when(c == nblk - 1)
        def _():
            kv0_ref[:, 0:ATT_OUT] = k
            kv0_ref[:, ATT_OUT:2 * ATT_OUT] = v.astype(F32)

        return carry

    lax.fori_loop(0, nblk, block0, 0)
    k0[0:n, :] = k0[tb:tb + n, :]
    v0[0:n, :] = v0[tb:tb + n, :]


def _attn_p_call(u_p, cos_t, sin_t, B, S, tb=ATT_TILE):
    n = ATT_N
    nt = S // tb
    d1, d2 = DIL_GROUPS[1][1], DIL_GROUPS[2][1]
    keep = [min(w, S) for w, _ in DIL_GROUPS]
    assert keep == [n, n * d1, tb] and tb == n * d2
    nsl = 2 * ATT_OUT // LANES
    kv_specs = [pl.BlockSpec((keep[0], 2 * ATT_OUT), lambda b, i: (b, 0))]
    kv_specs += [pl.BlockSpec((nsl, kp, LANES), lambda b, i: (0, b, 0)) for kp in keep[1:]]
    kv_shapes = [jax.ShapeDtypeStruct((B * keep[0], 2 * ATT_OUT), F32)]
    kv_shapes += [jax.ShapeDtypeStruct((nsl, B * kp, LANES), F32) for kp in keep[1:]]

    def buf(*shape):
        return pltpu.VMEM(shape, BF16)

    return pl.pallas_call(
        functools.partial(_attn_p_kernel, tb=tb),
        grid=(B, nt),
        in_specs=[pl.BlockSpec((tb, 3 * ATT_WIDTH), lambda b, i: (b * nt + i, COL_ATT // (3 * ATT_WIDTH))),
                  pl.BlockSpec((tb, LANES), lambda b, i: (i, 0)),
                  pl.BlockSpec((tb, LANES), lambda b, i: (i, 0))],
        out_specs=[pl.BlockSpec((tb, ATT_OUT), lambda b, i: (b * nt + i, 0))] + kv_specs,
        out_shape=[jax.ShapeDtypeStruct((B * S, ATT_OUT), BF16)] + kv_shapes,
        scratch_shapes=[buf(tb + n, ATT_OUT), buf(tb + n, ATT_OUT),
                        buf(d1, tb // d1 + n, ATT_OUT), buf(d1, tb // d1 + n, ATT_OUT),
                        buf(d2, 2 * n, ATT_OUT), buf(d2, 2 * n, ATT_OUT),
                        pltpu.VMEM((ATT_OUT // LANES, tb, LANES), F32), pltpu.VMEM((ATT_OUT // LANES, tb, LANES), F32)],
        compiler_params=_cp(("parallel", "arbitrary")),
    )(u_p, cos_t, sin_t)


def _attn_s_kernel(u_ref, c0_ref, c1_ref, c2_ref, cos_ref, sin_ref,
                   ya_ref, kv0_ref, kv1_ref, kv2_ref, *, bb, dt):
    cos = jnp.tile(cos_ref[...], (bb, 1))
    sin = jnp.tile(sin_ref[...], (bb, 1))
    masks = _head_masks((dt, ATT_OUT))
    nr = ATT_HPG * dt
    kv_refs = (kv0_ref, kv1_ref, kv2_ref)
    cache_refs = (c0_ref, c1_ref, c2_ref)
    qs, ks, vs = [], [], []
    for g in range(len(DIL_GROUPS)):
        c0 = g * ATT_WIDTH
        qs.append(_rope(u_ref[:, c0:c0 + ATT_OUT].astype(F32), cos, sin) * (ATT_HEAD_DIM ** -0.5))
        ks.append(_rope(u_ref[:, c0 + ATT_OUT:c0 + 2 * ATT_OUT].astype(F32), cos, sin))
        vs.append(u_ref[:, c0 + 2 * ATT_OUT:c0 + 3 * ATT_OUT].astype(F32))
    for bi in range(bb):
        rs = slice(bi * dt, (bi + 1) * dt)
        s_list, v_list = [], []
        for g, (win, dil) in enumerate(DIL_GROUPS):
            qg, kg, vg = qs[g][rs], ks[g][rs], vs[g][rs]
            kv_refs[g][bi] = jnp.concatenate([kg, vg], axis=1)
            qexp = jnp.concatenate([jnp.where(masks[h], qg, 0.0) for h in range(ATT_HPG)], axis=0).astype(BF16)
            kt = cache_refs[g][bi, 0].astype(BF16)
            vt = cache_refs[g][bi, 1].astype(BF16)
            L = kt.shape[1]
            t_r = _iota((nr, L), 0) % dt
            c_i = _iota((nr, L), 1)
            ok = ((c_i + dil * dt - t_r) % dil == 0) & (c_i >= t_r)
            s_list.append(jnp.where(ok, _dot(qexp, kt), -jnp.inf))
            v_list.append((vt, True))
            t_n = _iota((nr, 2 * dt), 0) % dt
            c_n = _iota((nr, 2 * dt), 1)
            okn = (c_n <= t_n) & ((t_n - c_n + dil * dt) % dil == 0)
            s_list.append(jnp.where(okn, _dot_nt(qexp, _pad_rows(kg, 2 * dt).astype(BF16)), -jnp.inf))
            v_list.append((_pad_rows(vg, 2 * dt).astype(BF16), False))
        m = functools.reduce(jnp.maximum, [jnp.max(s, axis=-1, keepdims=True) for s in s_list])
        den = jnp.zeros((nr, 1), F32)
        o = jnp.zeros((nr, ATT_OUT), F32)
        for s, (vv, transposed) in zip(s_list, v_list):
            p = jnp.exp(s - m)
            den = den + jnp.sum(p, axis=-1, keepdims=True)
            o = o + (_dot_nt(p.astype(BF16), vv) if transposed else _dot(p.astype(BF16), vv))
        o = o / den
        y = jnp.zeros((dt, ATT_OUT), F32)
        for h in range(ATT_HPG):
            y = jnp.where(masks[h], o[h * dt:(h + 1) * dt], y)
        ya_ref[rs, :] = y.astype(ya_ref.dtype)


def _attn_s_call(u_s, caches_t, cos_s, sin_s, layer, DB, DT, bb=2):
    rb = bb * DT
    in_specs = [pl.BlockSpec((rb, 3 * ATT_WIDTH), lambda i: (i, COL_ATT // (3 * ATT_WIDTH)))]
    in_specs += [pl.BlockSpec((None, bb, 2, ATT_OUT, c.shape[4]), lambda i: (layer, i, 0, 0, 0)) for c in caches_t]
    in_specs += [pl.BlockSpec((DT, LANES), lambda i: (0, 0))] * 2
    kv_spec = pl.BlockSpec((bb, DT, 2 * ATT_OUT), lambda i: (i, 0, 0))
    kv_sh = jax.ShapeDtypeStruct((DB, DT, 2 * ATT_OUT), F32)
    return pl.pallas_call(
        functools.partial(_attn_s_kernel, bb=bb, dt=DT),
        grid=(DB // bb,),
        in_specs=in_specs,
        out_specs=[pl.BlockSpec((rb, ATT_OUT), lambda i: (i, 0)), kv_spec, kv_spec, kv_spec],
        out_shape=[jax.ShapeDtypeStruct((DB * DT, ATT_OUT), BF16), kv_sh, kv_sh, kv_sh],
        compiler_params=_cp(("parallel",)),
    )(u_s, *caches_t, cos_s, sin_s)


def _hgrn_lb(lbp_ref, layer):
    p = lbp_ref[...]
    mx = jnp.max(p, axis=0, keepdims=True)
    e = jnp.exp(p - mx)
    sm = e / jnp.sum(e, axis=0, keepdims=True)
    lb = jnp.zeros((1, p.shape[1]), F32)
    for i in range(1, layer + 1):
        lb = lb + sm[i:i + 1]
    return lb


def _hgrn_gates(qr, fr, lb):
    q = _silu(qr)
    f = lb + (1.0 - lb) * jax.nn.sigmoid(fr)
    return q, jnp.log(f), 1.0 - f


def _hgrn_chunk_head(q, k, v, b, st):
    C = q.shape[0]
    mxu_rows = 2 * SUBLANES
    o_inter = _dot_nt(_pad_rows(q * jnp.exp(b), mxu_rows).astype(BF16), st.astype(BF16))[:C]
    r8 = _iota((SUBLANES, 1), 0)
    rows = []
    for tb in range(C // SUBLANES):
        qt = q[tb * SUBLANES:(tb + 1) * SUBLANES]
        bt = b[tb * SUBLANES:(tb + 1) * SUBLANES]
        acc = jnp.zeros((SUBLANES, q.shape[1]), F32)
        for s in range((tb + 1) * SUBLANES):
            e = jnp.exp(jnp.minimum(bt - b[s:s + 1], 0.0))
            a = jnp.sum(qt * k[s:s + 1] * e, axis=1, keepdims=True)
            if s >= tb * SUBLANES:
                a = jnp.where(r8 >= (s - tb * SUBLANES), a, 0.0)
            acc = acc + a * v[s:s + 1]
        rows.append(acc)
    o = o_inter + (jnp.concatenate(rows, axis=0) if len(rows) > 1 else rows[0])
    bl = b[C - 1:C]
    ke = k * jnp.exp(bl - b)
    upd = _dot_tn(_pad_rows(v, mxu_rows).astype(BF16), _pad_rows(ke, mxu_rows).astype(BF16))
    return o, st * jnp.exp(bl) + upd


def _hgrn_rows(qr, fr, ir, gr, lb, ng, st_get, st_set, C):
    q, logf, k = _hgrn_gates(qr, fr, lb)
    b = _cumsum_seg(logf, C)
    outs = []
    for h in range(HG_HEADS):
        sl = slice(h * HG_HEAD_DIM, (h + 1) * HG_HEAD_DIM)
        o, st_new = _hgrn_chunk_head(q[:, sl], k[:, sl], ir[:, sl], b[:, sl], st_get(h))
        st_set(h, st_new)
        outs.append(o * lax.rsqrt(jnp.mean(o * o, axis=-1, keepdims=True) + RMS_EPS))
    return jnp.concatenate(outs, axis=1) * ng * _silu(gr)


def _hgrn_p_kernel(q_ref, f_ref, i_ref, g_ref, lbp_ref, ng_ref, y_ref, st_ref, st_scr, *, tt, C, layer):
    t = pl.program_id(1)

    @pl.when(t == 0)
    def _():
        st_scr[...] = jnp.zeros_like(st_scr)

    lb = _hgrn_lb(lbp_ref, layer)
    ng = ng_ref[...]

    def chunk(c, carry):
        r0 = pl.multiple_of(c * C, C)
        rs = pl.ds(r0, C)

        def st_set(h, val):
            st_scr[h] = val

        y = _hgrn_rows(q_ref[rs, :].astype(F32), f_ref[rs, :].astype(F32), i_ref[rs, :].astype(F32),
                       g_ref[rs, :].astype(F32), lb, ng, lambda h: st_scr[h], st_set, C)
        y_ref[rs, :] = y.astype(y_ref.dtype)
        return carry

    lax.fori_loop(0, tt // C, chunk, 0)

    @pl.when(t == pl.num_programs(1) - 1)
    def _():
        for h in range(HG_HEADS):
            st_ref[h] = st_scr[h].T


def _hgrn_p_call(u, hg_lb, hg_norm_g, layer, B, S, tt=256):
    nt = S // tt
    wb = HG_WIDTH
    c0 = COL_HG // wb

    def col(o):
        return pl.BlockSpec((tt, wb), lambda b, t: (b * nt + t, c0 + o))

    return pl.pallas_call(
        functools.partial(_hgrn_p_kernel, tt=tt, C=HG_CHUNK, layer=layer),
        grid=(B, nt),
        in_specs=[col(0), col(1), col(2), col(3),
                  pl.BlockSpec(hg_lb.shape, lambda b, t: (0, 0)),
                  pl.BlockSpec((None, 1, wb), lambda b, t: (layer, 0, 0))],
        out_specs=[pl.BlockSpec((tt, wb), lambda b, t: (b * nt + t, 0)),
                   pl.BlockSpec((None, HG_HEADS, HG_HEAD_DIM, HG_HEAD_DIM), lambda b, t: (b, 0, 0, 0))],
        out_shape=[jax.ShapeDtypeStruct((B * S, wb), BF16),
                   jax.ShapeDtypeStruct((B, HG_HEADS, HG_HEAD_DIM, HG_HEAD_DIM), F32)],
        scratch_shapes=[pltpu.VMEM((HG_HEADS, HG_HEAD_DIM, HG_HEAD_DIM), F32)],
        compiler_params=_cp(("parallel", "arbitrary")),
    )(u, u, u, u, hg_lb, hg_norm_g.reshape(DEPTH, 1, wb))


def _hgrn_s_kernel(q_ref, f_ref, i_ref, g_ref, s0_ref, lbp_ref, ng_ref, y_ref, s1_ref,
                   qs, fs, is_, gs, ys, *, bb, dt, layer):
    lb = _hgrn_lb(lbp_ref, layer)
    ng = ng_ref[...]
    qs[...] = q_ref[...].astype(F32)
    fs[...] = f_ref[...].astype(F32)
    is_[...] = i_ref[...].astype(F32)
    gs[...] = g_ref[...].astype(F32)

    def seq(j, carry):
        rs = pl.ds(pl.multiple_of(j * dt, dt), dt)

        def st_set(h, val):
            s1_ref[j, h] = val.T

        ys[rs, :] = _hgrn_rows(qs[rs, :], fs[rs, :], is_[rs, :], gs[rs, :], lb, ng,
                               lambda h: s0_ref[j, h].T, st_set, dt)
        return carry

    lax.fori_loop(0, bb, seq, 0)
    y_ref[...] = ys[...].astype(y_ref.dtype)


def _hgrn_s_call(u, state, hg_lb, hg_norm_g, layer, row0, DB, DT, bb=8):
    wb = HG_WIDTH
    rb = bb * DT
    c0 = COL_HG // wb

    def col(o):
        return pl.BlockSpec((rb, wb), lambda i: (row0 // rb + i, c0 + o))

    st_spec = pl.BlockSpec((None, bb, HG_HEADS, HG_HEAD_DIM, HG_HEAD_DIM), lambda i: (layer, i, 0, 0, 0))
    return pl.pallas_call(
        functools.partial(_hgrn_s_kernel, bb=bb, dt=DT, layer=layer),
        grid=(DB // bb,),
        in_specs=[col(0), col(1), col(2), col(3), st_spec,
                  pl.BlockSpec(hg_lb.shape, lambda i: (0, 0)),
                  pl.BlockSpec((None, 1, wb), lambda i: (layer, 0, 0))],
        out_specs=[pl.BlockSpec((rb, wb), lambda i: (i, 0)),
                   pl.BlockSpec((bb, HG_HEADS, HG_HEAD_DIM, HG_HEAD_DIM), lambda i: (i, 0, 0, 0))],
        out_shape=[jax.ShapeDtypeStruct((DB * DT, wb), BF16),
                   jax.ShapeDtypeStruct((DB, HG_HEADS, HG_HEAD_DIM, HG_HEAD_DIM), F32)],
        scratch_shapes=[pltpu.VMEM((rb, wb), F32)] * 5,
        compiler_params=_cp(("parallel",)),
    )(u, u, u, u, state, hg_lb, hg_norm_g.reshape(DEPTH, 1, wb))


def _softplus(x):
    return jnp.maximum(x, 0.0) + jnp.log1p(jnp.exp(-jnp.abs(x)))


def _ssd_expand_mats():
    h_i = _iota((LANES, SSM_INNER), 0)
    e64 = (h_i == _iota((LANES, SSM_INNER), 1) // SSM_HEAD_DIM).astype(BF16)
    h_j = _iota((LANES, SSM_HEADS * LANES), 0)
    e128 = (h_j == _iota((LANES, SSM_HEADS * LANES), 1) // LANES).astype(BF16)
    return e64, e128


def _ssd_common(xc, z, dtr, prm, seq_len):
    C = xc.shape[0]
    dtb, alog, dsk, ng = prm
    xc = _silu(xc)
    xs = xc[:, :SSM_INNER]
    bm = xc[:, SSM_INNER:SSM_INNER + SSM_GROUPS * SSM_STATE]
    cm = xc[:, SSM_INNER + SSM_GROUPS * SSM_STATE:]
    dt = _softplus(dtr + dtb)
    dta = dt * (-jnp.exp(alog))
    r_i = _iota((C, C), 0)
    c_i = _iota((C, C), 1)
    same = (r_i // seq_len) == (c_i // seq_len)
    causal = same & (r_i >= c_i)
    cum = _dot01_left(causal.astype(BF16), dta)
    e64, e128 = _ssd_expand_mats()
    cum_e = _dot01_right(cum, e64)
    dt_e = _dot01_right(dt, e64)
    cum_c = _dot01_right(cum, e128)
    cum_t = cum.T
    xdt = xs * dt_e
    lane = _iota((C, LANES), 1)
    ys = []
    cbs = [_dot_nt(cm[:, g * SSM_STATE:(g + 1) * SSM_STATE].astype(BF16),
                   bm[:, g * SSM_STATE:(g + 1) * SSM_STATE].astype(BF16)) for g in range(SSM_GROUPS)]
    hpg = SSM_HEADS // SSM_GROUPS
    for hp in range(SSM_HEADS // 2):
        ms = []
        for h in (2 * hp, 2 * hp + 1):
            dec = jnp.exp(jnp.minimum(cum_c[:, h * LANES:h * LANES + C] - cum_t[h:h + 1, :], 0.0))
            ms.append(jnp.where(causal, cbs[h // hpg] * dec, 0.0))
        mcat = jnp.concatenate(ms, axis=1).astype(BF16)
        xp = xdt[:, hp * LANES:(hp + 1) * LANES]
        xbd = jnp.concatenate([jnp.where(lane < SSM_HEAD_DIM, xp, 0.0),
                               jnp.where(lane >= SSM_HEAD_DIM, xp, 0.0)], axis=0).astype(BF16)
        ys.append(_dot(mcat, xbd))
    y_intra = jnp.concatenate(ys, axis=1)
    return dict(xs=xs, bm=bm, cm=cm, cum=cum, cum_e=cum_e, cum_t=cum_t, xdt=xdt, y_intra=y_intra,
                z=z, dsk=dsk, ng=ng)


def _ssd_finish(y, it):
    y = y + it["dsk"] * it["xs"]
    y = y * _silu(it["z"])
    half = SSM_INNER // SSM_GROUPS
    outs = []
    for g in range(SSM_GROUPS):
        yg = y[:, g * half:(g + 1) * half]
        outs.append(yg * lax.rsqrt(jnp.mean(yg * yg, axis=-1, keepdims=True) + RMS_EPS))
    return jnp.concatenate(outs, axis=1) * it["ng"]


def _ssd_p_kernel(z_ref, xbc_ref, dt_ref, cw_ref, cb_ref, dtb_ref, alog_ref, dsk_ref, ng_ref,
                  y_ref, st_ref, st_scr, prev_scr, *, C):
    t = pl.program_id(1)

    @pl.when(t == 0)
    def _():
        st_scr[...] = jnp.zeros_like(st_scr)
        prev_scr[...] = jnp.zeros_like(prev_scr)

    x = xbc_ref[...].astype(F32)
    prev8 = prev_scr[...]
    cw = cw_ref[...]
    r8 = _iota((SUBLANES, 1), 0)
    xc = cb_ref[...] + cw[SSM_CONV - 1:SSM_CONV] * x
    for k in range(1, SSM_CONV):
        sh = pltpu.roll(x, k, 0)
        head = jnp.where(r8 < k, pltpu.roll(prev8, k, 0), sh[:SUBLANES])
        xc = xc + cw[SSM_CONV - 1 - k:SSM_CONV - k] * jnp.concatenate([head, sh[SUBLANES:]], axis=0)
    prev_scr[...] = x[C - SUBLANES:]
    it = _ssd_common(xc, z_ref[...].astype(F32), dt_ref[...].astype(F32)[:, :LANES],
                     (dtb_ref[...], alog_ref[...], dsk_ref[...], ng_ref[...]), C)
    half = SSM_INNER // SSM_GROUPS
    st = st_scr[...]
    cl_e = it["cum_e"][C - 1:C]
    y_inter = jnp.concatenate(
        [_dot(it["cm"][:, g * SSM_STATE:(g + 1) * SSM_STATE].astype(BF16), st[:, g * half:(g + 1) * half].astype(BF16))
         for g in range(SSM_GROUPS)], axis=1) * jnp.exp(it["cum_e"])
    w = (it["xdt"] * jnp.exp(cl_e - it["cum_e"])).astype(BF16)
    upd = jnp.concatenate(
        [_dot_tn(it["bm"][:, g * SSM_STATE:(g + 1) * SSM_STATE].astype(BF16), w[:, g * half:(g + 1) * half])
         for g in range(SSM_GROUPS)], axis=1)
    st_scr[...] = st * jnp.exp(cl_e) + upd
    y_ref[...] = _ssd_finish(it["y_intra"] + y_inter, it).astype(y_ref.dtype)

    @pl.when(t == pl.num_programs(1) - 1)
    def _():
        for j in range(SSM_INNER // LANES):
            st_ref[j * LANES:(j + 1) * LANES, :] = st_scr[:, j * LANES:(j + 1) * LANES].T


def _ssd_params(p, layer):
    def padl(v):
        return jnp.pad(v[layer], (0, LANES - SSM_HEADS)).reshape(1, LANES)

    return (p["conv_w"][layer], p["conv_b"][layer].reshape(1, CONV_DIM), padl(p["dt_bias"]), padl(p["a_log"]),
            jnp.repeat(p["d_skip"][layer], SSM_HEAD_DIM).reshape(1, SSM_INNER),
            p["ssm_norm_g"][layer].reshape(1, SSM_INNER))


def _full(a):
    nd = a.ndim
    return pl.BlockSpec(a.shape, lambda *_: (0,) * nd)


def _ssd_p_call(u, prm, B, S):
    C = SSD_CHUNK
    nt = S // C
    in_specs = [pl.BlockSpec((C, SSM_INNER), lambda b, t: (b * nt + t, COL_Z // SSM_INNER)),
                pl.BlockSpec((C, CONV_DIM), lambda b, t: (b * nt + t, COL_XBC // CONV_DIM)),
                pl.BlockSpec((C, HG_WIDTH), lambda b, t: (b * nt + t, COL_DT // HG_WIDTH))] + [_full(a) for a in prm]
    return pl.pallas_call(
        functools.partial(_ssd_p_kernel, C=C),
        grid=(B, nt),
        in_specs=in_specs,
        out_specs=[pl.BlockSpec((C, SSM_INNER), lambda b, t: (b * nt + t, 0)),
                   pl.BlockSpec((None, SSM_INNER, SSM_STATE), lambda b, t: (b, 0, 0))],
        out_shape=[jax.ShapeDtypeStruct((B * S, SSM_INNER), BF16),
                   jax.ShapeDtypeStruct((B, SSM_INNER, SSM_STATE), F32)],
        scratch_shapes=[pltpu.VMEM((SSM_STATE, SSM_INNER), F32), pltpu.VMEM((SUBLANES, CONV_DIM), F32)],
        compiler_params=_cp(("parallel", "arbitrary")),
    )(u, u, u, *prm)


def _ssd_s_kernel(z_ref, xbc_ref, dt_ref, c0_ref, s0_ref, cw_ref, cb_ref, dtb_ref, alog_ref, dsk_ref, ng_ref,
                  y_ref, s1_ref, *, bb, dt):
    C = bb * dt
    x = xbc_ref[...].astype(F32)
    c0 = c0_ref[...]
    c0 = jnp.concatenate([c0, jnp.zeros((bb, dt - c0.shape[1], CONV_DIM), F32)], axis=1).reshape(C, CONV_DIM)
    cw = cw_ref[...]
    tpos = _iota((C, 1), 0) % dt
    xc = cb_ref[...] + cw[SSM_CONV - 1:SSM_CONV] * x
    for k in range(1, SSM_CONV):
        term = jnp.where(tpos < k, pltpu.roll(c0, C + k - (SSM_CONV - 1), 0), pltpu.roll(x, k, 0))
        xc = xc + cw[SSM_CONV - 1 - k:SSM_CONV - k] * term
    it = _ssd_common(xc, z_ref[...].astype(F32), dt_ref[...].astype(F32)[:, :LANES],
                     (dtb_ref[...], alog_ref[...], dsk_ref[...], ng_ref[...]), dt)
    half = SSM_INNER // SSM_GROUPS
    e64t = (_iota((SSM_INNER, LANES), 0) // SSM_HEAD_DIM == _iota((SSM_INNER, LANES), 1)).astype(BF16)
    cum_et = _dot01_left(e64t, it["cum_t"])
    row_seq = _iota((C, 1), 0) // dt
    lane_seq = _iota((1, C), 1) // dt
    cl_col = jnp.concatenate([jnp.broadcast_to(it["cum_e"][(j + 1) * dt - 1:(j + 1) * dt], (dt, SSM_INNER))
                              for j in range(bb)], axis=0)
    w_t = (it["xdt"] * jnp.exp(cl_col - it["cum_e"])).T
    y_inter = jnp.zeros((C, SSM_INNER), F32)
    for j in range(bb):
        s_j = s0_ref[j]
        yj = jnp.concatenate(
            [_dot_nt(it["cm"][:, g * SSM_STATE:(g + 1) * SSM_STATE].astype(BF16),
                     s_j[g * half:(g + 1) * half].astype(BF16)) for g in range(SSM_GROUPS)], axis=1)
        y_inter = jnp.where(row_seq == j, yj, y_inter)
        wj = jnp.where(lane_seq == j, w_t, 0.0).astype(BF16)
        upd = jnp.concatenate(
            [_dot(wj[g * half:(g + 1) * half], it["bm"][:, g * SSM_STATE:(g + 1) * SSM_STATE].astype(BF16))
             for g in range(SSM_GROUPS)], axis=0)
        dec = jnp.exp(cum_et[:, (j + 1) * dt - 1:(j + 1) * dt])
        s1_ref[j] = s_j * dec + upd
    y_ref[...] = _ssd_finish(it["y_intra"] + y_inter * jnp.exp(it["cum_e"]), it).astype(y_ref.dtype)


def _ssd_s_call(u, conv_state, ssm_state, prm, layer, row0, DB, DT, bb=16):
    rb = bb * DT
    in_specs = [pl.BlockSpec((rb, SSM_INNER), lambda i: (row0 // rb + i, COL_Z // SSM_INNER)),
                pl.BlockSpec((rb, CONV_DIM), lambda i: (row0 // rb + i, COL_XBC // CONV_DIM)),
                pl.BlockSpec((rb, HG_WIDTH), lambda i: (row0 // rb + i, COL_DT // HG_WIDTH)),
                pl.BlockSpec((None, bb, SSM_CONV - 1, CONV_DIM), lambda i: (layer, i, 0, 0)),
                pl.BlockSpec((None, bb, SSM_INNER, SSM_STATE), lambda i: (layer, i, 0, 0))] + [_full(a) for a in prm]
    return pl.pallas_call(
        functools.partial(_ssd_s_kernel, bb=bb, dt=DT),
        grid=(DB // bb,),
        in_specs=in_specs,
        out_specs=[pl.BlockSpec((rb, SSM_INNER), lambda i: (i, 0)),
                   pl.BlockSpec((bb, SSM_INNER, SSM_STATE), lambda i: (i, 0, 0))],
        out_shape=[jax.ShapeDtypeStruct((DB * DT, SSM_INNER), BF16),
                   jax.ShapeDtypeStruct((DB, SSM_INNER, SSM_STATE), F32)],
        compiler_params=_cp(("parallel",)),
    )(u, u, u, conv_state, ssm_state, *prm)


def _layer_norm(v, g, b):
    mu = jnp.mean(v, axis=-1, keepdims=True)
    d = v - mu
    var = jnp.mean(d * d, axis=-1, keepdims=True)
    return d * lax.rsqrt(var + LN_EPS) * g + b


def _merge_kernel(x_ref, gp_ref, gs_ref, yap_ref, yas_ref, yhp_ref, yhs_ref, ysp_ref, yss_ref,
                  g1_ref, sh2_ref, sc2_ref, wa_ref, wh_ref, ws_ref, wo_ref, lng_ref, lnb_ref, wr_ref, br_ref,
                  x1_ref, h2_ref, ri_ref, rg_ref, cnt_ref, carry, *, n_prompt_tiles, tm):
    i = pl.program_id(0)

    @pl.when(i == 0)
    def _():
        carry[...] = jnp.zeros_like(carry)

    is_p = i < n_prompt_tiles
    ya = jnp.where(is_p, yap_ref[...], yas_ref[...])
    yh = jnp.where(is_p, yhp_ref[...], yhs_ref[...])
    ys = jnp.where(is_p, ysp_ref[...], yss_ref[...])
    gts = jax.nn.sigmoid(jnp.where(is_p, gp_ref[...], gs_ref[...]).astype(F32))
    m = (gts[:, :D_MODEL] * _dot(ya, wa_ref[...]) + gts[:, D_MODEL:2 * D_MODEL] * _dot(yh, wh_ref[...])
         + gts[:, 2 * D_MODEL:] * _dot(ys, ws_ref[...]))
    mix = _dot(m.astype(BF16), wo_ref[...])
    tg = tm // SUBLANES
    x = x_ref[...]
    v = ALPHA_DN * x + g1_ref[...] * mix.reshape(tg, SUBLANES, D_MODEL)
    x1 = _layer_norm(v, lng_ref[...], lnb_ref[...])
    x1_ref[...] = x1
    h2 = (x1 * (1.0 + sc2_ref[...]) + sh2_ref[...]).reshape(tm, D_MODEL)
    h2_ref[...] = h2
    logits = _dot_f32(h2, wr_ref[...]) + br_ref[...]
    lane = _iota(logits.shape, 1).astype(F32)
    cur = logits
    vals, idxs = [], []
    for _ in range(TOP_K):
        mval = jnp.max(cur, axis=-1, keepdims=True)
        idx = jnp.min(jnp.where(cur == mval, lane, float(N_EXPERTS)), axis=-1, keepdims=True)
        vals.append(mval)
        idxs.append(idx)
        cur = jnp.where(lane == idx, -jnp.inf, cur)
    ex = [jnp.exp(vv - vals[0]) for vv in vals]
    den = ex[0] + ex[1] + ex[2] + ex[3]
    onehot = jnp.zeros(logits.shape, F32)
    for idx in idxs:
        onehot = onehot + (lane == idx).astype(F32)
    strict = (_iota((tm, tm), 0) > _iota((tm, tm), 1)).astype(BF16)
    prefix = _dot(strict, onehot.astype(BF16)) + carry[...]
    carry[...] = carry[...] + jnp.sum(onehot, axis=0, keepdims=True)
    cnt_ref[...] = carry[...]
    l128 = _iota((tm, LANES), 1)
    ri = jnp.zeros((tm, LANES), I32)
    rg = jnp.zeros((tm, LANES), F32)
    for k in range(TOP_K):
        rank = jnp.sum(jnp.where(lane == idxs[k], prefix, 0.0), axis=-1, keepdims=True).astype(I32)
        ri = jnp.where(l128 == k, idxs[k].astype(I32), ri)
        ri = jnp.where(l128 == TOP_K + k, rank, ri)
        rg = jnp.where(l128 == k, ex[k] / den, rg)
    ri_ref[...] = ri
    rg_ref[...] = rg


def _merge_call(x3, u_p, u_s, ya_p, ya_s, yh_p, yh_s, ys_p, ys_s, mod4, wts, layer, n_prompt_rows, tm=256):
    G, _, D = x3.shape
    T = G * SUBLANES
    tg = tm // SUBLANES
    npt = n_prompt_rows // tm
    gcol = COL_GATE // (3 * D)

    def prow(w, c=0):
        return pl.BlockSpec((tm, w), lambda i: (jnp.minimum(i, npt - 1), c))

    def srow(w, c=0):
        return pl.BlockSpec((tm, w), lambda i: (jnp.maximum(i - npt, 0), c))

    def modspec(c):
        return pl.BlockSpec((None, tg, 1, D), lambda i: (layer, i, 0, c))

    in_specs = [pl.BlockSpec((tg, SUBLANES, D), lambda i: (i, 0, 0)),
                prow(3 * D, gcol), srow(3 * D, gcol),
                prow(ATT_OUT), srow(ATT_OUT), prow(HG_WIDTH), srow(HG_WIDTH), prow(SSM_INNER), srow(SSM_INNER),
                modspec(2), modspec(3), modspec(4)] + [_full(a) for a in wts]
    return pl.pallas_call(
        functools.partial(_merge_kernel, n_prompt_tiles=npt, tm=tm),
        grid=(T // tm,),
        in_specs=in_specs,
        out_specs=[pl.BlockSpec((tg, SUBLANES, D), lambda i: (i, 0, 0)),
                   pl.BlockSpec((tm, D), lambda i: (i, 0)),
                   pl.BlockSpec((tm, LANES), lambda i: (i, 0)),
                   pl.BlockSpec((tm, LANES), lambda i: (i, 0)),
                   pl.BlockSpec((1, N_EXPERTS), lambda i: (0, 0))],
        out_shape=[jax.ShapeDtypeStruct((G, SUBLANES, D), F32),
                   jax.ShapeDtypeStruct((T, D), F32),
                   jax.ShapeDtypeStruct((T, LANES), I32),
                   jax.ShapeDtypeStruct((T, LANES), F32),
                   jax.ShapeDtypeStruct((1, N_EXPERTS), F32)],
        scratch_shapes=[pltpu.VMEM((1, N_EXPERTS), F32)],
        compiler_params=_cp(("arbitrary",)),
    )(x3, u_p, u_s, ya_p, ya_s, yh_p, yh_s, ys_p, ys_s, mod4, mod4, mod4, *wts)


def _dispatch_kernel(dest_ref, h_ref, xs_in_ref, xs_ref, sem, *, td):
    del xs_in_ref

    def row_copy(t, k):
        return pltpu.make_async_copy(h_ref.at[pl.ds(t, 1)], xs_ref.at[pl.ds(dest_ref[t * TOP_K + k], 1)], sem)

    def body(t, carry):
        for k in range(TOP_K):
            row_copy(t, k).start()
        return carry

    lax.fori_loop(0, td, body, 0)

    def drain(t, carry):
        for k in range(TOP_K):
            row_copy(t, k).wait()
        return carry

    lax.fori_loop(0, td, drain, 0)


def _dispatch_call(dest_flat, h2, xs_zero, td=256):
    T, D = h2.shape
    return pl.pallas_call(
        functools.partial(_dispatch_kernel, td=td),
        grid=(T // td,),
        in_specs=[pl.BlockSpec((td * TOP_K,), lambda i: (i,), memory_space=pltpu.SMEM),
                  pl.BlockSpec((td, D), lambda i: (i, 0)),
                  pl.BlockSpec(memory_space=pl.ANY)],
        out_specs=pl.BlockSpec(memory_space=pl.ANY),
        out_shape=jax.ShapeDtypeStruct(xs_zero.shape, xs_zero.dtype),
        scratch_shapes=[pltpu.SemaphoreType.DMA(())],
        input_output_aliases={2: 0},
        compiler_params=_cp(("arbitrary",)),
    )(dest_flat, h2, xs_zero)


def _expert_kernel(be_ref, nv_ref, x_ref, wgu_ref, bgu_ref, wdn_ref, bdn_ref, o_ref, wgu_bf, wdn_bf):
    j = pl.program_id(0)
    changed = jnp.logical_or(j == 0, be_ref[j] != be_ref[jnp.maximum(j - 1, 0)])

    @pl.when(changed)
    def _():
        wgu_bf[...] = wgu_ref[...].astype(BF16)
        wdn_bf[...] = wdn_ref[...].astype(BF16)

    @pl.when(j < nv_ref[0])
    def _():
        gu = _dot(x_ref[...].astype(BF16), wgu_bf[...]) + bgu_ref[...]
        g = jnp.minimum(gu[:, :D_FF], SWIGLU_LIMIT)
        up = jnp.clip(gu[:, D_FF:], -SWIGLU_LIMIT, SWIGLU_LIMIT)
        hid = (up + 1.0) * g * jax.nn.sigmoid(SWIGLU_ALPHA * g)
        o_ref[...] = _dot(hid.astype(BF16), wdn_bf[...]) + bdn_ref[...]

    @pl.when(j >= nv_ref[0])
    def _():
        o_ref[...] = jnp.zeros_like(o_ref)


def _expert_call(block_e, nvalid, xs, w_gu, b_gu4, w_dn, b_dn4, layer):
    NS, D = xs.shape
    bm = MOE_BM
    grid_spec = pltpu.PrefetchScalarGridSpec(
        num_scalar_prefetch=2,
        grid=(NS // bm,),
        in_specs=[pl.BlockSpec((bm, D), lambda j, be, nv: (jnp.minimum(j, nv[0] - 1), 0)),
                  pl.BlockSpec((None, None, D, 2 * D_FF), lambda j, be, nv: (layer, be[j], 0, 0)),
                  pl.BlockSpec((None, None, 1, 2 * D_FF), lambda j, be, nv: (layer, be[j], 0, 0)),
                  pl.BlockSpec((None, None, D_FF, D), lambda j, be, nv: (layer, be[j], 0, 0)),
                  pl.BlockSpec((None, None, 1, D), lambda j, be, nv: (layer, be[j], 0, 0))],
        out_specs=pl.BlockSpec((bm, D), lambda j, be, nv: (j, 0)),
        scratch_shapes=[pltpu.VMEM((D, 2 * D_FF), BF16), pltpu.VMEM((D_FF, D), BF16)])
    return pl.pallas_call(
        _expert_kernel,
        grid_spec=grid_spec,
        out_shape=jax.ShapeDtypeStruct((NS, D), F32),
        compiler_params=_cp(("arbitrary",)),
    )(block_e, nvalid, xs, w_gu, b_gu4, w_dn, b_dn4)


def _combine_kernel(dest_ref, rg_ref, x1_ref, g2_ref, lng_ref, lnb_ref, outs_ref, x2_ref, buf, sem, *, tc):
    def row_copy(t, k):
        return pltpu.make_async_copy(outs_ref.at[pl.ds(dest_ref[t * TOP_K + k], 1)], buf.at[k, pl.ds(t, 1)], sem)

    def body(t, carry):
        for k in range(TOP_K):
            row_copy(t, k).start()
        return carry

    lax.fori_loop(0, tc, body, 0)

    def drain(t, carry):
        for k in range(TOP_K):
            row_copy(t, k).wait()
        return carry

    lax.fori_loop(0, tc, drain, 0)
    rg = rg_ref[...]
    ff = rg[:, 0:1] * buf[0]
    for k in range(1, TOP_K):
        ff = ff + rg[:, k:k + 1] * buf[k]
    tg = tc // SUBLANES
    v = ALPHA_DN * x1_ref[...] + g2_ref[...] * ff.reshape(tg, SUBLANES, D_MODEL)
    x2_ref[...] = _layer_norm(v, lng_ref[...], lnb_ref[...])


def _combine_call(dest_flat, rg, x1_3, mod4, lng, lnb, outs, layer, tc=256):
    G, _, D = x1_3.shape
    T = G * SUBLANES
    tg = tc // SUBLANES
    return pl.pallas_call(
        functools.partial(_combine_kernel, tc=tc),
        grid=(T // tc,),
        in_specs=[pl.BlockSpec((tc * TOP_K,), lambda i: (i,), memory_space=pltpu.SMEM),
                  pl.BlockSpec((tc, LANES), lambda i: (i, 0)),
                  pl.BlockSpec((tg, SUBLANES, D), lambda i: (i, 0, 0)),
                  pl.BlockSpec((None, tg, 1, D), lambda i: (layer, i, 0, 5)),
                  _full(lng), _full(lnb),
                  pl.BlockSpec(memory_space=pl.ANY)],
        out_specs=pl.BlockSpec((tg, SUBLANES, D), lambda i: (i, 0, 0)),
        out_shape=jax.ShapeDtypeStruct((G, SUBLANES, D), F32),
        scratch_shapes=[pltpu.VMEM((TOP_K, tc, D), F32), pltpu.SemaphoreType.DMA(())],
        compiler_params=_cp(("arbitrary",)),
    )(dest_flat, rg, x1_3, mod4, lng, lnb, outs)


def _moe_plan(ri, counts):
    bm = MOE_BM
    T = ri.shape[0]
    cnt = counts.reshape(N_EXPERTS).astype(I32)
    padded = (cnt + bm - 1) // bm * bm
    pends = jnp.cumsum(padded)
    pstarts = pends - padded
    eidx = ri[:, :TOP_K]
    rank = ri[:, TOP_K:2 * TOP_K]
    dest = (pstarts[eidx] + rank).reshape(T * TOP_K)
    nb = (T * TOP_K) // bm + N_EXPERTS
    blk = jnp.arange(nb, dtype=I32)
    block_e = jnp.minimum(jnp.sum((pends[None, :] // bm <= blk[:, None]).astype(I32), axis=1), N_EXPERTS - 1)
    nvalid = (pends[-1] // bm).astype(I32).reshape(1)
    last_e = block_e[jnp.maximum(nvalid[0] - 1, 0)]
    block_e = jnp.where(jnp.arange(nb) < nvalid[0], block_e, last_e)
    return dest, block_e, nvalid, nb * bm


def _pack_w_in(w_in):
    depth, D, _ = w_in.shape
    sizes = [ATT_WIDTH] * 3 + [HG_WIDTH] * 4 + [SSM_INNER, CONV_DIM, SSM_HEADS, 3 * D_MODEL]
    pts = [0]
    for s in sizes:
        pts.append(pts[-1] + s)
    att = [w_in[:, :, pts[part] + g * ATT_OUT:pts[part] + (g + 1) * ATT_OUT]
           for g in range(len(DIL_GROUPS)) for part in range(3)]
    att_hg = jnp.concatenate(att + [w_in[:, :, pts[3]:pts[7]]], axis=2)
    z = w_in[:, :, pts[7]:pts[8]]
    xbc = w_in[:, :, pts[8]:pts[9]]
    dt = w_in[:, :, pts[9]:pts[10]]
    gates = w_in[:, :, pts[10]:pts[11]]

    def zeros(n):
        return jnp.zeros((depth, D, n), w_in.dtype)

    packed = jnp.concatenate([att_hg, dt, zeros(COL_Z - COL_DT - SSM_HEADS), z, zeros(COL_XBC - COL_Z - SSM_INNER),
                              xbc, gates], axis=2)
    assert packed.shape[2] == NP_COLS
    return packed.astype(BF16)


def kernel(x_prompt, x_sample, c_prompt, c_sample, cache_kv_w128, cache_kv_w512, cache_kv_w2048, state_hgrn, state_ssm, state_conv, w_in, w_br_att, w_br_hg, w_br_ssm, w_out, conv_w, conv_b, dt_bias, a_log, d_skip, ssm_norm_g, hg_lb, hg_norm_g, ln1_g, ln1_b, ln2_g, ln2_b, w_ada, b_ada, w_router, b_router, w_gu, b_gu, w_dn, b_dn):
    B, S, D = x_prompt.shape
    DB, DT, _ = x_sample.shape
    depth = w_in.shape[0]
    TP = B * S
    TS = DB * DT
    T = TP + TS
    G = T // SUBLANES
    p = dict(conv_w=conv_w, conv_b=conv_b, dt_bias=dt_bias, a_log=a_log, d_skip=d_skip, ssm_norm_g=ssm_norm_g)

    x2 = jnp.concatenate([x_prompt.reshape(TP, D), x_sample.reshape(TS, D)], axis=0)
    x3 = x2.reshape(G, SUBLANES, D)
    c_groups = jnp.concatenate([jnp.repeat(c_prompt, S // SUBLANES, axis=0),
                                jnp.repeat(c_sample, DT // SUBLANES, axis=0)], axis=0)
    mod4 = _ada_call(c_groups, w_ada, b_ada).reshape(depth, G, 1, 6 * D)

    w_in_p = _pack_w_in(w_in)
    cos_p, sin_p = _rope_tables(jnp.arange(S, dtype=I32))
    cos_s, sin_s = _rope_tables(PAST_LEN + jnp.arange(DT, dtype=I32))

    def cache_t(c):
        return jnp.transpose(c, (0, 1, 3, 4, 5, 2)).reshape(depth, DB, 2, ATT_OUT, c.shape[2])

    kv_views = (cache_t(cache_kv_w128), cache_t(cache_kv_w512), cache_t(cache_kv_w2048))
    ssm_state_v = state_ssm.reshape(depth, DB, SSM_INNER, SSM_STATE)
    b_gu4 = b_gu.reshape(depth, N_EXPERTS, 1, 2 * D_FF)
    b_dn4 = b_dn.reshape(depth, N_EXPERTS, 1, D)

    outs_p = [[] for _ in range(6)]
    outs_s = [[] for _ in range(6)]
    for l in range(depth):
        if l > 0:
            x2 = x3.reshape(T, D)
        u_p = _inproj_p_call(x2, mod4, w_in_p, l, B, S)
        u_s = _inproj_s_call(x3, mod4, w_in_p, l, TP // SUBLANES, TS // SUBLANES)

        ya_p, kvp0, kvp1, kvp2 = _attn_p_call(u_p, cos_p, sin_p, B, S)
        outs_p[0].append(kvp0.reshape(B, -1, 2, ATT_HPG, ATT_HEAD_DIM))
        for g, kv in ((1, kvp1), (2, kvp2)):
            outs_p[g].append(jnp.transpose(kv, (1, 0, 2)).reshape(B, -1, 2, ATT_HPG, ATT_HEAD_DIM))
        ya_s, kv0, kv1, kv2 = _attn_s_call(u_s, kv_views, cos_s, sin_s, l, DB, DT)
        for g, kv in enumerate((kv0, kv1, kv2)):
            outs_s[g].append(kv.reshape(DB, DT, 2, ATT_HPG, ATT_HEAD_DIM))

        yh_p, hg_st_p = _hgrn_p_call(u_p, hg_lb, hg_norm_g, l, B, S)
        yh_s, hg_st_s = _hgrn_s_call(u_s, state_hgrn, hg_lb, hg_norm_g, l, 0, DB, DT)
        outs_p[3].append(hg_st_p)
        outs_s[3].append(hg_st_s)

        prm = _ssd_params(p, l)
        ys_p, ssm_st_p = _ssd_p_call(u_p, prm, B, S)
        ys_s, ssm_st_s = _ssd_s_call(u_s, state_conv, ssm_state_v, prm, l, 0, DB, DT)
        outs_p[4].append(ssm_st_p.reshape(B, SSM_HEADS, SSM_HEAD_DIM, SSM_STATE))
        outs_s[4].append(ssm_st_s.reshape(DB, SSM_HEADS, SSM_HEAD_DIM, SSM_STATE))
        xbc_cols = slice(COL_XBC, COL_XBC + CONV_DIM)
        outs_p[5].append(u_p.reshape(B, S, -1)[:, S - (SSM_CONV - 1):, xbc_cols].astype(F32))
        outs_s[5].append(u_s.reshape(DB, DT, -1)[:, DT - (SSM_CONV - 1):, xbc_cols].astype(F32))

        wts = (w_br_att[l].astype(BF16), w_br_hg[l].astype(BF16), w_br_ssm[l].astype(BF16), w_out[l].astype(BF16),
               ln1_g[l].reshape(1, D), ln1_b[l].reshape(1, D), w_router[l], b_router[l].reshape(1, N_EXPERTS))
        x1_3, h2, ri, rg, counts = _merge_call(x3, u_p, u_s, ya_p, ya_s, yh_p, yh_s, ys_p, ys_s, mod4, wts, l, TP)

        dest, block_e, nvalid, ns = _moe_plan(ri, counts)
        xs = _dispatch_call(dest, h2, jnp.zeros((ns, D), F32))
        eo = _expert_call(block_e, nvalid, xs, w_gu, b_gu4, w_dn, b_dn4, l)
        x3 = _combine_call(dest, rg, x1_3, mod4, ln2_g[l].reshape(1, D), ln2_b[l].reshape(1, D), eo, l)

    xf = x3.reshape(T, D)
    y_prompt = xf[:TP].reshape(B, S, D)
    y_sample = xf[TP:].reshape(DB, DT, D)
    st_p = [jnp.stack(o) for o in outs_p]
    st_s = [jnp.stack(o) for o in outs_s]
    return (y_prompt, y_sample, *st_p, *st_s)
```

```python
import functools
import math

import jax
import jax.numpy as jnp
from jax import lax
from jax.experimental import pallas as pl
from jax.experimental.pallas import tpu as pltpu

F32 = jnp.float32
BF16 = jnp.bfloat16
I32 = jnp.int32

D_MODEL = 1024
DEPTH = 4
PAST_LEN = 2048
ATT_HEAD_DIM = 64
ATT_HPG = 4
DIL_GROUPS = ((128, 1), (512, 4), (2048, 16))
ATT_WIDTH = 768
ATT_OUT = 256
ROPE_THETA = 10000.0
HG_HEADS = 6
HG_HEAD_DIM = 128
HG_WIDTH = 768
SSM_HEADS = 16
SSM_HEAD_DIM = 64
SSM_INNER = 1024
SSM_GROUPS = 2
SSM_STATE = 128
SSM_CONV = 4
CONV_DIM = 1536
N_EXPERTS = 32
TOP_K = 4
D_FF = 1024
SWIGLU_LIMIT = 7.0
SWIGLU_ALPHA = 1.702
ALPHA_DN = (2 * DEPTH) ** 0.25
LN_EPS = 1e-5
RMS_EPS = 1e-6
ATT_N = 128

COL_ATT = 0
COL_HG = 2304
COL_DT = 5376
COL_Z = 6144
COL_XBC = 7680
COL_GATE = 9216
NP_COLS = 12288

LANES = 128
SUBLANES = 8
VMEM_LIMIT = 56 * 1024 * 1024
MOE_BM = 256
ATT_TILE = ATT_N * DIL_GROUPS[2][1]
HG_CHUNK = 16
SSD_CHUNK = 128
DMA_ISSUE_UNROLL = 4
DMA_WAIT_UNROLL = 16


def _cp(sem, vmem=VMEM_LIMIT):
    return pltpu.CompilerParams(dimension_semantics=sem, vmem_limit_bytes=vmem)


def _silu(x):
    return x * jax.nn.sigmoid(x)


def _dot(a, b):
    return jnp.dot(a, b, preferred_element_type=F32)


def _dot_nt(a, b):
    return lax.dot_general(a, b, (((1,), (1,)), ((), ())), preferred_element_type=F32)


def _dot_tn(a, b):
    return lax.dot_general(a, b, (((0,), (0,)), ((), ())), preferred_element_type=F32)


def _split3(x):
    hi = x.astype(BF16)
    r1 = x - hi.astype(F32)
    mid = r1.astype(BF16)
    lo = (r1 - mid.astype(F32)).astype(BF16)
    return hi, mid, lo


def _dot01_left(m01, x):
    hi, mid, lo = _split3(x)
    return _dot(m01, hi) + _dot(m01, mid) + _dot(m01, lo)


def _dot01_right(x, m01):
    hi, mid, lo = _split3(x)
    return _dot(hi, m01) + _dot(mid, m01) + _dot(lo, m01)


def _dot_f32(a, b):
    ah, am, al = _split3(a)
    bh, bm, bl = _split3(b)
    return (_dot(ah, bh) + _dot(ah, bm) + _dot(am, bh)) + (_dot(ah, bl) + _dot(al, bh) + _dot(am, bm))


def _iota(shape, dim):
    return lax.broadcasted_iota(I32, shape, dim)


def _cumsum_seg(x, seg):
    pos = _iota((x.shape[0], 1), 0) % seg
    sh = 1
    while sh < seg:
        x = x + jnp.where(pos >= sh, pltpu.roll(x, sh, 0), 0.0)
        sh *= 2
    return x


def _pad_rows(x, rows):
    if x.shape[0] >= rows:
        return x
    return jnp.concatenate([x, jnp.zeros((rows - x.shape[0],) + x.shape[1:], x.dtype)], axis=0)


def _pick(n, options):
    for o in options:
        if n % o == 0:
            return o
    raise ValueError(f"no tile in {options} divides {n}")


def _ada_kernel(c_ref, w_ref, b_ref, o_ref):
    s = _silu(c_ref[...])
    o_ref[...] = _dot(s.astype(BF16), w_ref[...].astype(BF16)) + b_ref[...]


def _ada_call(c_groups, w_ada, b_ada):
    G, D = c_groups.shape
    depth = w_ada.shape[0]
    nj = w_ada.shape[2] // D
    return pl.pallas_call(
        _ada_kernel,
        grid=(depth, nj),
        in_specs=[pl.BlockSpec((G, D), lambda l, j: (0, 0)),
                  pl.BlockSpec((None, D, D), lambda l, j: (l, 0, j)),
                  pl.BlockSpec((None, 1, D), lambda l, j: (l, 0, j))],
        out_specs=pl.BlockSpec((None, G, D), lambda l, j: (l, 0, j)),
        out_shape=jax.ShapeDtypeStruct((depth, G, nj * D), F32),
        compiler_params=_cp(("parallel", "parallel")),
    )(c_groups, w_ada, b_ada.reshape(depth, 1, nj * D))


def _inproj_kernel(x_ref, sh_ref, sc_ref, w_ref, o_ref, h_scr):
    @pl.when(pl.program_id(1) == 0)
    def _():
        h = x_ref[...] * (1.0 + sc_ref[...]) + sh_ref[...]
        h_scr[...] = h.reshape(h_scr.shape).astype(BF16)

    o_ref[...] = _dot(h_scr[...], w_ref[...]).astype(o_ref.dtype)


def _inproj_s_call(x3, mod4, w_in_p, layer, group0, n_groups, tn=ATT_WIDTH):
    _, _, D = x3.shape
    tg = _pick(n_groups, (128, 64, 32, 16))
    tm = tg * SUBLANES
    npc = w_in_p.shape[2]
    g0 = group0 // tg
    return pl.pallas_call(
        _inproj_kernel,
        grid=(n_groups // tg, npc // tn),
        in_specs=[pl.BlockSpec((tg, SUBLANES, D), lambda i, j: (g0 + i, 0, 0)),
                  pl.BlockSpec((None, tg, 1, D), lambda i, j: (layer, g0 + i, 0, 0)),
                  pl.BlockSpec((None, tg, 1, D), lambda i, j: (layer, g0 + i, 0, 1)),
                  pl.BlockSpec((None, D, tn), lambda i, j: (layer, 0, j))],
        out_specs=pl.BlockSpec((tm, tn), lambda i, j: (i, j)),
        out_shape=jax.ShapeDtypeStruct((n_groups * SUBLANES, npc), BF16),
        scratch_shapes=[pltpu.VMEM((tm, D), BF16)],
        compiler_params=_cp(("parallel", "arbitrary")),
    )(x3, mod4, mod4, w_in_p)


def _inproj_p_kernel(x_ref, sh_ref, sc_ref, w_ref, o_ref, h_scr, xcol, *, tm):
    j = pl.program_id(1)
    ncol = xcol.shape[0]

    @pl.when(j == 0)
    def _():
        sc1 = 1.0 + sc_ref[...]
        sh = sh_ref[...]
        h_scr[0] = (x_ref[...] * sc1 + sh).astype(BF16)
        for c in range(ncol):
            xcol[c] = x_ref[:, c * LANES:(c + 1) * LANES]
        for gi in (1, 2):
            d = DIL_GROUPS[gi][1]
            n = tm // d
            for r in range(d):
                xr = jnp.concatenate([xcol[c, pl.ds(r, n, stride=d), :] for c in range(ncol)], axis=1)
                h_scr[gi, r * n:(r + 1) * n, :] = (xr * sc1 + sh).astype(BF16)

    sel = jnp.where(j == 1, 1, jnp.where(j == 2, 2, 0))
    o_ref[...] = _dot(h_scr[sel], w_ref[...]).astype(o_ref.dtype)


def _inproj_p_call(x2, mod4, w_in_p, layer, B, S, tm=ATT_TILE, tn=ATT_WIDTH):
    TP, D = x2.shape[0], x2.shape[1]
    TP = B * S
    npc = w_in_p.shape[2]
    tiles_per_seq = S // tm
    gps = S // SUBLANES

    def modspec(c):
        return pl.BlockSpec((None, None, 1, D), lambda i, j: (layer, (i // tiles_per_seq) * gps, 0, c))

    return pl.pallas_call(
        functools.partial(_inproj_p_kernel, tm=tm),
        grid=(TP // tm, npc // tn),
        in_specs=[pl.BlockSpec((tm, D), lambda i, j: (i, 0)), modspec(0), modspec(1),
                  pl.BlockSpec((None, D, tn), lambda i, j: (layer, 0, j))],
        out_specs=pl.BlockSpec((tm, tn), lambda i, j: (i, j)),
        out_shape=jax.ShapeDtypeStruct((TP, npc), BF16),
        scratch_shapes=[pltpu.VMEM((3, tm, D), BF16), pltpu.VMEM((D // LANES, tm, LANES), F32)],
        compiler_params=_cp(("parallel", "arbitrary")),
    )(x2, mod4, mod4, w_in_p)


def _rope(x, cos, sin_signed):
    outs = []
    for blk in range(x.shape[1] // LANES):
        xx = x[:, blk * LANES:(blk + 1) * LANES]
        lane = _iota(xx.shape, 1)
        first_half = (lane % ATT_HEAD_DIM) < (ATT_HEAD_DIM // 2)
        rot = jnp.where(first_half, pltpu.roll(xx, LANES - 32, 1), pltpu.roll(xx, 32, 1))
        outs.append(xx * cos + rot * sin_signed)
    return jnp.concatenate(outs, axis=1)


def _rope_tables(pos):
    half = ATT_HEAD_DIM // 2
    inv = ROPE_THETA ** (-jnp.arange(half, dtype=F32) / half)
    ang = pos.astype(F32)[:, None] * inv[None, :]
    cos = jnp.cos(ang)
    sin = jnp.sin(ang)
    reps = LANES // ATT_HEAD_DIM
    return (jnp.tile(jnp.concatenate([cos, cos], -1), (1, reps)),
            jnp.tile(jnp.concatenate([-sin, sin], -1), (1, reps)))


def _head_masks(shape):
    lane = _iota(shape, 1)
    return [(lane // ATT_HEAD_DIM) == h for h in range(ATT_HPG)]


def _attn_block(qc, kk, vv, mask, masks):
    o_acc = jnp.zeros(qc.shape, F32)
    l_acc = jnp.zeros(qc.shape, F32)
    for h in range(ATT_HPG):
        qh = jnp.where(masks[h], qc, jnp.zeros_like(qc))
        s = jnp.where(mask, _dot_nt(qh, kk), -jnp.inf)
        m = jnp.max(s, axis=-1, keepdims=True)
        p = jnp.exp(s - m)
        den = jnp.sum(p, axis=-1, keepdims=True)
        oh = _dot(p.astype(BF16), vv) / den
        o_acc = jnp.where(masks[h], oh, o_acc)
        l_acc = jnp.where(masks[h], m + jnp.log(den), l_acc)
    return o_acc, l_acc


def _st_slabs(ref, rows, val):
    for c in range(ref.shape[0]):
        ref[c, rows, :] = val[:, c * LANES:(c + 1) * LANES]


def _ld_slabs(ref, rows):
    return jnp.concatenate([ref[c, rows, :] for c in range(ref.shape[0])], axis=1)


def _lse_merge(o1, l1, o2, l2):
    m = jnp.maximum(l1, l2)
    e1 = jnp.exp(l1 - m)
    e2 = jnp.exp(l2 - m)
    den = e1 + e2
    return (e1 * o1 + e2 * o2) / den, m + jnp.log(den)


def _attn_p_kernel(u_ref, cos_ref, sin_ref, ya_ref, kv0_ref, kv1_ref, kv2_ref,
                   k0, v0, kp0, vp0, kp1, vp1, kp2, vp2, o_scr, l_scr, *, tb):
    n = ATT_N
    i = pl.program_id(1)
    scale = ATT_HEAD_DIM ** -0.5
    d1, d2 = DIL_GROUPS[1][1], DIL_GROUPS[2][1]
    n1, n2 = tb // d1, tb // d2

    wr = i % 2
    rd = 1 - wr

    @pl.when(i == 0)
    def _():
        for ref in (kp0, vp0, kp1, vp1, kp2, vp2):
            ref[1] = jnp.zeros(ref.shape[1:], BF16)

    masks = _head_masks((n, ATT_OUT))
    row = _iota((n, 2 * n), 0)
    col = _iota((n, 2 * n), 1)
    band = (col >= row) & (col <= row + n)
    first_mask = band & (col >= jnp.where(i > 0, 0, n))

    def qkv(rows, g, cos, sin):
        c0 = g * ATT_WIDTH
        q = (_rope(u_ref[rows, c0:c0 + ATT_OUT].astype(F32), cos, sin) * scale).astype(BF16)
        k = _rope(u_ref[rows, c0 + ATT_OUT:c0 + 2 * ATT_OUT].astype(F32), cos, sin)
        v = u_ref[rows, c0 + 2 * ATT_OUT:c0 + 3 * ATT_OUT]
        return q, k, v

    def stream2(r, carry):
        tok = pl.ds(r, n2, stride=d2)
        q, k, v = qkv(pl.ds(pl.multiple_of(r * n2, n2), n2), 2, cos_ref[tok, :], sin_ref[tok, :])
        kb = k.astype(BF16)
        _st_slabs(kv2_ref, tok, jnp.concatenate([k, v.astype(F32)], axis=1))
        o, l = _attn_block(q, jnp.concatenate([kp2[rd, r], kb], axis=0), jnp.concatenate([vp2[rd, r], v], axis=0),
                           first_mask, masks)
        _st_slabs(o_scr, tok, o)
        _st_slabs(l_scr, tok, l)
        kp2[wr, r] = kb
        vp2[wr, r] = v
        return carry

    lax.fori_loop(0, d2, stream2, 0)

    def stream1(r, carry):
        tok_all = pl.ds(r, n1, stride=d1)
        q, k, v = qkv(pl.ds(pl.multiple_of(r * n1, n1), n1), 1, cos_ref[tok_all, :], sin_ref[tok_all, :])
        kall = jnp.concatenate([kp1[rd, r], k.astype(BF16)], axis=0)
        vall = jnp.concatenate([vp1[rd, r], v], axis=0)
        last = pl.ds(r, n, stride=d1)
        _st_slabs(kv1_ref, last, jnp.concatenate([k[n1 - n:], v[n1 - n:].astype(F32)], axis=1))
        for c in range(n1 // n):
            o, l = _attn_block(q[c * n:(c + 1) * n], kall[c * n:(c + 2) * n], vall[c * n:(c + 2) * n],
                               first_mask if c == 0 else band, masks)
            tok = pl.ds(r + c * n * d1, n, stride=d1)
            om, lm = _lse_merge(o, l, _ld_slabs(o_scr, tok), _ld_slabs(l_scr, tok))
            _st_slabs(o_scr, tok, om)
            _st_slabs(l_scr, tok, lm)
        kp1[wr, r] = kall[n1:]
        vp1[wr, r] = vall[n1:]
        return carry

    lax.fori_loop(0, d1, stream1, 0)

    nblk = tb // n

    def block0(c, first):
        rows = pl.ds(0, n) if first else pl.ds(pl.multiple_of(c * n, n), n)
        q, k, v = qkv(rows, 0, cos_ref[rows, :], sin_ref[rows, :])
        kb = k.astype(BF16)
        k0[rows, :] = kb
        v0[rows, :] = v
        if first:
            kk = jnp.concatenate([kp0[rd], kb], axis=0)
            vv = jnp.concatenate([vp0[rd], v], axis=0)
            mask = first_mask
        else:
            both = pl.ds(pl.multiple_of(c * n - n, n), 2 * n)
            kk, vv, mask = k0[both, :], v0[both, :], band
        o, l = _attn_block(q, kk, vv, mask, masks)
        om, _ = _lse_merge(o, l, _ld_slabs(o_scr, rows), _ld_slabs(l_scr, rows))
        ya_ref[rows, :] = om.astype(ya_ref.dtype)
        if not first:
            @pl.when(c == nblk - 1)
            def _():
                kv0_ref[:, 0:ATT_OUT] = k
                kv0_ref[:, ATT_OUT:2 * ATT_OUT] = v.astype(F32)
                kp0[wr] = kb
                vp0[wr] = v

    block0(0, True)

    def block0_body(c, carry):
        block0(c, False)
        return carry

    lax.fori_loop(1, nblk, block0_body, 0)


def _attn_p_call(u_p, cos_t, sin_t, B, S, tb=ATT_TILE):
    n = ATT_N
    nt = S // tb
    d1, d2 = DIL_GROUPS[1][1], DIL_GROUPS[2][1]
    keep = [min(w, S) for w, _ in DIL_GROUPS]
    assert keep == [n, n * d1, tb] and tb == n * d2
    nsl = 2 * ATT_OUT // LANES
    kv_specs = [pl.BlockSpec((keep[0], 2 * ATT_OUT), lambda b, i: (b, 0))]
    kv_specs += [pl.BlockSpec((nsl, kp, LANES), lambda b, i: (0, b, 0)) for kp in keep[1:]]
    kv_shapes = [jax.ShapeDtypeStruct((B * keep[0], 2 * ATT_OUT), F32)]
    kv_shapes += [jax.ShapeDtypeStruct((nsl, B * kp, LANES), F32) for kp in keep[1:]]

    def buf(*shape):
        return pltpu.VMEM(shape, BF16)

    return pl.pallas_call(
        functools.partial(_attn_p_kernel, tb=tb),
        grid=(B, nt),
        in_specs=[pl.BlockSpec((tb, 3 * ATT_WIDTH), lambda b, i: (b * nt + i, COL_ATT // (3 * ATT_WIDTH))),
                  pl.BlockSpec((tb, LANES), lambda b, i: (i, 0)),
                  pl.BlockSpec((tb, LANES), lambda b, i: (i, 0))],
        out_specs=[pl.BlockSpec((tb, ATT_OUT), lambda b, i: (b * nt + i, 0))] + kv_specs,
        out_shape=[jax.ShapeDtypeStruct((B * S, ATT_OUT), BF16)] + kv_shapes,
        scratch_shapes=[buf(tb, ATT_OUT), buf(tb, ATT_OUT),
                        buf(2, n, ATT_OUT), buf(2, n, ATT_OUT),
                        buf(2, d1, n, ATT_OUT), buf(2, d1, n, ATT_OUT),
                        buf(2, d2, n, ATT_OUT), buf(2, d2, n, ATT_OUT),
                        pltpu.VMEM((ATT_OUT // LANES, tb, LANES), F32), pltpu.VMEM((ATT_OUT // LANES, tb, LANES), F32)],
        compiler_params=_cp(("parallel", "arbitrary")),
    )(u_p, cos_t, sin_t)


def _attn_s_kernel(u_ref, c0_ref, c1_ref, c2_ref, cos_ref, sin_ref,
                   ya_ref, kv0_ref, kv1_ref, kv2_ref, *, bb, dt):
    cos = jnp.tile(cos_ref[...], (bb, 1))
    sin = jnp.tile(sin_ref[...], (bb, 1))
    masks = _head_masks((dt, ATT_OUT))
    nr = ATT_HPG * dt
    kv_refs = (kv0_ref, kv1_ref, kv2_ref)
    cache_refs = (c0_ref, c1_ref, c2_ref)
    qs, ks, vs = [], [], []
    for g in range(len(DIL_GROUPS)):
        c0 = g * ATT_WIDTH
        qs.append(_rope(u_ref[:, c0:c0 + ATT_OUT].astype(F32), cos, sin) * (ATT_HEAD_DIM ** -0.5))
        ks.append(_rope(u_ref[:, c0 + ATT_OUT:c0 + 2 * ATT_OUT].astype(F32), cos, sin))
        vs.append(u_ref[:, c0 + 2 * ATT_OUT:c0 + 3 * ATT_OUT].astype(F32))
    for bi in range(bb):
        rs = slice(bi * dt, (bi + 1) * dt)
        s_list, v_list = [], []
        for g, (win, dil) in enumerate(DIL_GROUPS):
            qg, kg, vg = qs[g][rs], ks[g][rs], vs[g][rs]
            kv_refs[g][bi] = jnp.concatenate([kg, vg], axis=1)
            qexp = jnp.concatenate([jnp.where(masks[h], qg, 0.0) for h in range(ATT_HPG)], axis=0).astype(BF16)
            kt = cache_refs[g][bi, 0].astype(BF16)
            vt = cache_refs[g][bi, 1].astype(BF16)
            L = kt.shape[1]
            t_r = _iota((nr, L), 0) % dt
            c_i = _iota((nr, L), 1)
            ok = ((c_i + dil * dt - t_r) % dil == 0) & (c_i >= t_r)
            s_list.append(jnp.where(ok, _dot(qexp, kt), -jnp.inf))
            v_list.append((vt, True))
            t_n = _iota((nr, 2 * dt), 0) % dt
            c_n = _iota((nr, 2 * dt), 1)
            okn = (c_n <= t_n) & ((t_n - c_n + dil * dt) % dil == 0)
            s_list.append(jnp.where(okn, _dot_nt(qexp, _pad_rows(kg, 2 * dt).astype(BF16)), -jnp.inf))
            v_list.append((_pad_rows(vg, 2 * dt).astype(BF16), False))
        m = functools.reduce(jnp.maximum, [jnp.max(s, axis=-1, keepdims=True) for s in s_list])
        den = jnp.zeros((nr, 1), F32)
        o = jnp.zeros((nr, ATT_OUT), F32)
        for s, (vv, transposed) in zip(s_list, v_list):
            p = jnp.exp(s - m)
            den = den + jnp.sum(p, axis=-1, keepdims=True)
            o = o + (_dot_nt(p.astype(BF16), vv) if transposed else _dot(p.astype(BF16), vv))
        o = o / den
        y = jnp.zeros((dt, ATT_OUT), F32)
        for h in range(ATT_HPG):
            y = jnp.where(masks[h], o[h * dt:(h + 1) * dt], y)
        ya_ref[rs, :] = y.astype(ya_ref.dtype)


def _attn_s_call(u_s, caches_t, cos_s, sin_s, layer, DB, DT, bb=2):
    rb = bb * DT
    in_specs = [pl.BlockSpec((rb, 3 * ATT_WIDTH), lambda i: (i, COL_ATT // (3 * ATT_WIDTH)))]
    in_specs += [pl.BlockSpec((None, bb, 2, ATT_OUT, c.shape[4]), lambda i: (layer, i, 0, 0, 0)) for c in caches_t]
    in_specs += [pl.BlockSpec((DT, LANES), lambda i: (0, 0))] * 2
    kv_spec = pl.BlockSpec((bb, DT, 2 * ATT_OUT), lambda i: (i, 0, 0))
    kv_sh = jax.ShapeDtypeStruct((DB, DT, 2 * ATT_OUT), F32)
    return pl.pallas_call(
        functools.partial(_attn_s_kernel, bb=bb, dt=DT),
        grid=(DB // bb,),
        in_specs=in_specs,
        out_specs=[pl.BlockSpec((rb, ATT_OUT), lambda i: (i, 0)), kv_spec, kv_spec, kv_spec],
        out_shape=[jax.ShapeDtypeStruct((DB * DT, ATT_OUT), BF16), kv_sh, kv_sh, kv_sh],
        compiler_params=_cp(("parallel",)),
    )(u_s, *caches_t, cos_s, sin_s)


def _hgrn_lb(lbp_ref, layer):
    p = lbp_ref[...]
    mx = jnp.max(p, axis=0, keepdims=True)
    e = jnp.exp(p - mx)
    sm = e / jnp.sum(e, axis=0, keepdims=True)
    lb = jnp.zeros((1, p.shape[1]), F32)
    for i in range(1, layer + 1):
        lb = lb + sm[i:i + 1]
    return lb


def _hgrn_gates(qr, fr, lb):
    q = _silu(qr)
    f = lb + (1.0 - lb) * jax.nn.sigmoid(fr)
    return q, jnp.log(f), 1.0 - f


def _hgrn_chunk_head(q, k, v, b, st):
    C = q.shape[0]
    mxu_rows = 2 * SUBLANES
    o_inter = _dot_nt(_pad_rows(q * jnp.exp(b), mxu_rows).astype(BF16), st.astype(BF16))[:C]
    r8 = _iota((SUBLANES, 1), 0)
    rows = []
    for tb in range(C // SUBLANES):
        qt = q[tb * SUBLANES:(tb + 1) * SUBLANES]
        bt = b[tb * SUBLANES:(tb + 1) * SUBLANES]
        acc = jnp.zeros((SUBLANES, q.shape[1]), F32)
        for s in range((tb + 1) * SUBLANES):
            e = jnp.exp(jnp.minimum(bt - b[s:s + 1], 0.0))
            a = jnp.sum(qt * k[s:s + 1] * e, axis=1, keepdims=True)
            if s >= tb * SUBLANES:
                a = jnp.where(r8 >= (s - tb * SUBLANES), a, 0.0)
            acc = acc + a * v[s:s + 1]
        rows.append(acc)
    o = o_inter + (jnp.concatenate(rows, axis=0) if len(rows) > 1 else rows[0])
    bl = b[C - 1:C]
    ke = k * jnp.exp(bl - b)
    upd = _dot_tn(_pad_rows(v, mxu_rows).astype(BF16), _pad_rows(ke, mxu_rows).astype(BF16))
    return o, st * jnp.exp(bl) + upd


def _hgrn_rows(qr, fr, ir, gr, lb, ng, st_get, st_set, C):
    q, logf, k = _hgrn_gates(qr, fr, lb)
    b = _cumsum_seg(logf, C)
    outs = []
    for h in range(HG_HEADS):
        sl = slice(h * HG_HEAD_DIM, (h + 1) * HG_HEAD_DIM)
        o, st_new = _hgrn_chunk_head(q[:, sl], k[:, sl], ir[:, sl], b[:, sl], st_get(h))
        st_set(h, st_new)
        outs.append(o * lax.rsqrt(jnp.mean(o * o, axis=-1, keepdims=True) + RMS_EPS))
    return jnp.concatenate(outs, axis=1) * ng * _silu(gr)


def _hgrn_p_kernel(q_ref, f_ref, i_ref, g_ref, lbp_ref, ng_ref, y_ref, st_ref, st_scr, *, tt, C, layer):
    t = pl.program_id(1)

    @pl.when(t == 0)
    def _():
        st_scr[...] = jnp.zeros_like(st_scr)

    lb = _hgrn_lb(lbp_ref, layer)
    ng = ng_ref[...]

    def chunk(c, carry):
        r0 = pl.multiple_of(c * C, C)
        rs = pl.ds(r0, C)

        def st_set(h, val):
            st_scr[h] = val

        y = _hgrn_rows(q_ref[rs, :].astype(F32), f_ref[rs, :].astype(F32), i_ref[rs, :].astype(F32),
                       g_ref[rs, :].astype(F32), lb, ng, lambda h: st_scr[h], st_set, C)
        y_ref[rs, :] = y.astype(y_ref.dtype)
        return carry

    lax.fori_loop(0, tt // C, chunk, 0)

    @pl.when(t == pl.num_programs(1) - 1)
    def _():
        for h in range(HG_HEADS):
            st_ref[h] = st_scr[h].T


def _hgrn_p_call(u, hg_lb, hg_norm_g, layer, B, S, tt=256):
    nt = S // tt
    wb = HG_WIDTH
    c0 = COL_HG // wb

    def col(o):
        return pl.BlockSpec((tt, wb), lambda b, t: (b * nt + t, c0 + o))

    return pl.pallas_call(
        functools.partial(_hgrn_p_kernel, tt=tt, C=HG_CHUNK, layer=layer),
        grid=(B, nt),
        in_specs=[col(0), col(1), col(2), col(3),
                  pl.BlockSpec(hg_lb.shape, lambda b, t: (0, 0)),
                  pl.BlockSpec((None, 1, wb), lambda b, t: (layer, 0, 0))],
        out_specs=[pl.BlockSpec((tt, wb), lambda b, t: (b * nt + t, 0)),
                   pl.BlockSpec((None, HG_HEADS, HG_HEAD_DIM, HG_HEAD_DIM), lambda b, t: (b, 0, 0, 0))],
        out_shape=[jax.ShapeDtypeStruct((B * S, wb), BF16),
                   jax.ShapeDtypeStruct((B, HG_HEADS, HG_HEAD_DIM, HG_HEAD_DIM), F32)],
        scratch_shapes=[pltpu.VMEM((HG_HEADS, HG_HEAD_DIM, HG_HEAD_DIM), F32)],
        compiler_params=_cp(("parallel", "arbitrary")),
    )(u, u, u, u, hg_lb, hg_norm_g.reshape(DEPTH, 1, wb))


def _hgrn_s_kernel(q_ref, f_ref, i_ref, g_ref, s0_ref, lbp_ref, ng_ref, y_ref, s1_ref,
                   qs, fs, is_, gs, ys, *, bb, dt, layer):
    lb = _hgrn_lb(lbp_ref, layer)
    ng = ng_ref[...]
    qs[...] = q_ref[...].astype(F32)
    fs[...] = f_ref[...].astype(F32)
    is_[...] = i_ref[...].astype(F32)
    gs[...] = g_ref[...].astype(F32)

    def seq(j, carry):
        rs = pl.ds(pl.multiple_of(j * dt, dt), dt)

        def st_set(h, val):
            s1_ref[j, h] = val.T

        ys[rs, :] = _hgrn_rows(qs[rs, :], fs[rs, :], is_[rs, :], gs[rs, :], lb, ng,
                               lambda h: s0_ref[j, h].T, st_set, dt)
        return carry

    lax.fori_loop(0, bb, seq, 0)
    y_ref[...] = ys[...].astype(y_ref.dtype)


def _hgrn_s_call(u, state, hg_lb, hg_norm_g, layer, row0, DB, DT, bb=8):
    wb = HG_WIDTH
    rb = bb * DT
    c0 = COL_HG // wb

    def col(o):
        return pl.BlockSpec((rb, wb), lambda i: (row0 // rb + i, c0 + o))

    st_spec = pl.BlockSpec((None, bb, HG_HEADS, HG_HEAD_DIM, HG_HEAD_DIM), lambda i: (layer, i, 0, 0, 0))
    return pl.pallas_call(
        functools.partial(_hgrn_s_kernel, bb=bb, dt=DT, layer=layer),
        grid=(DB // bb,),
        in_specs=[col(0), col(1), col(2), col(3), st_spec,
                  pl.BlockSpec(hg_lb.shape, lambda i: (0, 0)),
                  pl.BlockSpec((None, 1, wb), lambda i: (layer, 0, 0))],
        out_specs=[pl.BlockSpec((rb, wb), lambda i: (i, 0)),
                   pl.BlockSpec((bb, HG_HEADS, HG_HEAD_DIM, HG_HEAD_DIM), lambda i: (i, 0, 0, 0))],
        out_shape=[jax.ShapeDtypeStruct((DB * DT, wb), BF16),
                   jax.ShapeDtypeStruct((DB, HG_HEADS, HG_HEAD_DIM, HG_HEAD_DIM), F32)],
        scratch_shapes=[pltpu.VMEM((rb, wb), F32)] * 5,
        compiler_params=_cp(("parallel",)),
    )(u, u, u, u, state, hg_lb, hg_norm_g.reshape(DEPTH, 1, wb))


def _softplus(x):
    return jnp.maximum(x, 0.0) + jnp.log1p(jnp.exp(-jnp.abs(x)))


def _ssd_expand_mats():
    h_i = _iota((LANES, SSM_INNER), 0)
    e64 = (h_i == _iota((LANES, SSM_INNER), 1) // SSM_HEAD_DIM).astype(BF16)
    h_j = _iota((LANES, SSM_HEADS * LANES), 0)
    e128 = (h_j == _iota((LANES, SSM_HEADS * LANES), 1) // LANES).astype(BF16)
    return e64, e128


def _ssd_common(xc, z, dtr, prm, seq_len):
    C = xc.shape[0]
    dtb, alog, dsk, ng = prm
    xc = _silu(xc)
    xs = xc[:, :SSM_INNER]
    bm = xc[:, SSM_INNER:SSM_INNER + SSM_GROUPS * SSM_STATE]
    cm = xc[:, SSM_INNER + SSM_GROUPS * SSM_STATE:]
    dt = _softplus(dtr + dtb)
    dta = dt * (-jnp.exp(alog))
    r_i = _iota((C, C), 0)
    c_i = _iota((C, C), 1)
    same = (r_i // seq_len) == (c_i // seq_len)
    causal = same & (r_i >= c_i)
    cum = _dot01_left(causal.astype(BF16), dta)
    e64, e128 = _ssd_expand_mats()
    cum_e = _dot01_right(cum, e64)
    dt_e = _dot01_right(dt, e64)
    cum_c = _dot01_right(cum, e128)
    cum_t = cum.T
    xdt = xs * dt_e
    lane = _iota((C, LANES), 1)
    ys = []
    cbs = [_dot_nt(cm[:, g * SSM_STATE:(g + 1) * SSM_STATE].astype(BF16),
                   bm[:, g * SSM_STATE:(g + 1) * SSM_STATE].astype(BF16)) for g in range(SSM_GROUPS)]
    hpg = SSM_HEADS // SSM_GROUPS
    for hp in range(SSM_HEADS // 2):
        ms = []
        for h in (2 * hp, 2 * hp + 1):
            dec = jnp.exp(jnp.minimum(cum_c[:, h * LANES:h * LANES + C] - cum_t[h:h + 1, :], 0.0))
            ms.append(jnp.where(causal, cbs[h // hpg] * dec, 0.0))
        mcat = jnp.concatenate(ms, axis=1).astype(BF16)
        xp = xdt[:, hp * LANES:(hp + 1) * LANES]
        xbd = jnp.concatenate([jnp.where(lane < SSM_HEAD_DIM, xp, 0.0),
                               jnp.where(lane >= SSM_HEAD_DIM, xp, 0.0)], axis=0).astype(BF16)
        ys.append(_dot(mcat, xbd))
    y_intra = jnp.concatenate(ys, axis=1)
    return dict(xs=xs, bm=bm, cm=cm, cum=cum, cum_e=cum_e, cum_t=cum_t, xdt=xdt, y_intra=y_intra,
                z=z, dsk=dsk, ng=ng)


def _ssd_finish(y, it):
    y = y + it["dsk"] * it["xs"]
    y = y * _silu(it["z"])
    half = SSM_INNER // SSM_GROUPS
    outs = []
    for g in range(SSM_GROUPS):
        yg = y[:, g * half:(g + 1) * half]
        outs.append(yg * lax.rsqrt(jnp.mean(yg * yg, axis=-1, keepdims=True) + RMS_EPS))
    return jnp.concatenate(outs, axis=1) * it["ng"]


def _ssd_p_kernel(z_ref, xbc_ref, dt_ref, cw_ref, cb_ref, dtb_ref, alog_ref, dsk_ref, ng_ref,
                  y_ref, st_ref, st_scr, prev_scr, *, C):
    t = pl.program_id(1)

    @pl.when(t == 0)
    def _():
        st_scr[...] = jnp.zeros_like(st_scr)
        prev_scr[...] = jnp.zeros_like(prev_scr)

    x = xbc_ref[...].astype(F32)
    prev8 = prev_scr[...]
    cw = cw_ref[...]
    r8 = _iota((SUBLANES, 1), 0)
    xc = cb_ref[...] + cw[SSM_CONV - 1:SSM_CONV] * x
    for k in range(1, SSM_CONV):
        sh = pltpu.roll(x, k, 0)
        head = jnp.where(r8 < k, pltpu.roll(prev8, k, 0), sh[:SUBLANES])
        xc = xc + cw[SSM_CONV - 1 - k:SSM_CONV - k] * jnp.concatenate([head, sh[SUBLANES:]], axis=0)
    prev_scr[...] = x[C - SUBLANES:]
    it = _ssd_common(xc, z_ref[...].astype(F32), dt_ref[...].astype(F32)[:, :LANES],
                     (dtb_ref[...], alog_ref[...], dsk_ref[...], ng_ref[...]), C)
    half = SSM_INNER // SSM_GROUPS
    st = st_scr[...]
    cl_e = it["cum_e"][C - 1:C]
    y_inter = jnp.concatenate(
        [_dot(it["cm"][:, g * SSM_STATE:(g + 1) * SSM_STATE].astype(BF16), st[:, g * half:(g + 1) * half].astype(BF16))
         for g in range(SSM_GROUPS)], axis=1) * jnp.exp(it["cum_e"])
    w = (it["xdt"] * jnp.exp(cl_e - it["cum_e"])).astype(BF16)
    upd = jnp.concatenate(
        [_dot_tn(it["bm"][:, g * SSM_STATE:(g + 1) * SSM_STATE].astype(BF16), w[:, g * half:(g + 1) * half])
         for g in range(SSM_GROUPS)], axis=1)
    st_scr[...] = st * jnp.exp(cl_e) + upd
    y_ref[...] = _ssd_finish(it["y_intra"] + y_inter, it).astype(y_ref.dtype)

    @pl.when(t == pl.num_programs(1) - 1)
    def _():
        for j in range(SSM_INNER // LANES):
            st_ref[j * LANES:(j + 1) * LANES, :] = st_scr[:, j * LANES:(j + 1) * LANES].T


def _ssd_params(p, layer):
    def padl(v):
        return jnp.pad(v[layer], (0, LANES - SSM_HEADS)).reshape(1, LANES)

    return (p["conv_w"][layer], p["conv_b"][layer].reshape(1, CONV_DIM), padl(p["dt_bias"]), padl(p["a_log"]),
            jnp.repeat(p["d_skip"][layer], SSM_HEAD_DIM).reshape(1, SSM_INNER),
            p["ssm_norm_g"][layer].reshape(1, SSM_INNER))


def _full(a):
    nd = a.ndim
    return pl.BlockSpec(a.shape, lambda *_: (0,) * nd)


def _ssd_p_call(u, prm, B, S):
    C = SSD_CHUNK
    nt = S // C
    in_specs = [pl.BlockSpec((C, SSM_INNER), lambda b, t: (b * nt + t, COL_Z // SSM_INNER)),
                pl.BlockSpec((C, CONV_DIM), lambda b, t: (b * nt + t, COL_XBC // CONV_DIM)),
                pl.BlockSpec((C, HG_WIDTH), lambda b, t: (b * nt + t, COL_DT // HG_WIDTH))] + [_full(a) for a in prm]
    return pl.pallas_call(
        functools.partial(_ssd_p_kernel, C=C),
        grid=(B, nt),
        in_specs=in_specs,
        out_specs=[pl.BlockSpec((C, SSM_INNER), lambda b, t: (b * nt + t, 0)),
                   pl.BlockSpec((None, SSM_INNER, SSM_STATE), lambda b, t: (b, 0, 0))],
        out_shape=[jax.ShapeDtypeStruct((B * S, SSM_INNER), BF16),
                   jax.ShapeDtypeStruct((B, SSM_INNER, SSM_STATE), F32)],
        scratch_shapes=[pltpu.VMEM((SSM_STATE, SSM_INNER), F32), pltpu.VMEM((SUBLANES, CONV_DIM), F32)],
        compiler_params=_cp(("parallel", "arbitrary")),
    )(u, u, u, *prm)


def _ssd_s_kernel(z_ref, xbc_ref, dt_ref, c0_ref, s0_ref, cw_ref, cb_ref, dtb_ref, alog_ref, dsk_ref, ng_ref,
                  y_ref, s1_ref, *, bb, dt):
    C = bb * dt
    x = xbc_ref[...].astype(F32)
    c0 = c0_ref[...]
    c0 = jnp.concatenate([c0, jnp.zeros((bb, dt - c0.shape[1], CONV_DIM), F32)], axis=1).reshape(C, CONV_DIM)
    cw = cw_ref[...]
    tpos = _iota((C, 1), 0) % dt
    xc = cb_ref[...] + cw[SSM_CONV - 1:SSM_CONV] * x
    for k in range(1, SSM_CONV):
        term = jnp.where(tpos < k, pltpu.roll(c0, C + k - (SSM_CONV - 1), 0), pltpu.roll(x, k, 0))
        xc = xc + cw[SSM_CONV - 1 - k:SSM_CONV - k] * term
    it = _ssd_common(xc, z_ref[...].astype(F32), dt_ref[...].astype(F32)[:, :LANES],
                     (dtb_ref[...], alog_ref[...], dsk_ref[...], ng_ref[...]), dt)
    half = SSM_INNER // SSM_GROUPS
    e64t = (_iota((SSM_INNER, LANES), 0) // SSM_HEAD_DIM == _iota((SSM_INNER, LANES), 1)).astype(BF16)
    cum_et = _dot01_left(e64t, it["cum_t"])
    row_seq = _iota((C, 1), 0) // dt
    lane_seq = _iota((1, C), 1) // dt
    cl_col = jnp.concatenate([jnp.broadcast_to(it["cum_e"][(j + 1) * dt - 1:(j + 1) * dt], (dt, SSM_INNER))
                              for j in range(bb)], axis=0)
    w_t = (it["xdt"] * jnp.exp(cl_col - it["cum_e"])).T
    y_inter = jnp.zeros((C, SSM_INNER), F32)
    for j in range(bb):
        s_j = s0_ref[j]
        yj = jnp.concatenate(
            [_dot_nt(it["cm"][:, g * SSM_STATE:(g + 1) * SSM_STATE].astype(BF16),
                     s_j[g * half:(g + 1) * half].astype(BF16)) for g in range(SSM_GROUPS)], axis=1)
        y_inter = jnp.where(row_seq == j, yj, y_inter)
        wj = jnp.where(lane_seq == j, w_t, 0.0).astype(BF16)
        upd = jnp.concatenate(
            [_dot(wj[g * half:(g + 1) * half], it["bm"][:, g * SSM_STATE:(g + 1) * SSM_STATE].astype(BF16))
             for g in range(SSM_GROUPS)], axis=0)
        dec = jnp.exp(cum_et[:, (j + 1) * dt - 1:(j + 1) * dt])
        s1_ref[j] = s_j * dec + upd
    y_ref[...] = _ssd_finish(it["y_intra"] + y_inter * jnp.exp(it["cum_e"]), it).astype(y_ref.dtype)


def _ssd_s_call(u, conv_state, ssm_state, prm, layer, row0, DB, DT, bb=16):
    rb = bb * DT
    in_specs = [pl.BlockSpec((rb, SSM_INNER), lambda i: (row0 // rb + i, COL_Z // SSM_INNER)),
                pl.BlockSpec((rb, CONV_DIM), lambda i: (row0 // rb + i, COL_XBC // CONV_DIM)),
                pl.BlockSpec((rb, HG_WIDTH), lambda i: (row0 // rb + i, COL_DT // HG_WIDTH)),
                pl.BlockSpec((None, bb, SSM_CONV - 1, CONV_DIM), lambda i: (layer, i, 0, 0)),
                pl.BlockSpec((None, bb, SSM_INNER, SSM_STATE), lambda i: (layer, i, 0, 0))] + [_full(a) for a in prm]
    return pl.pallas_call(
        functools.partial(_ssd_s_kernel, bb=bb, dt=DT),
        grid=(DB // bb,),
        in_specs=in_specs,
        out_specs=[pl.BlockSpec((rb, SSM_INNER), lambda i: (i, 0)),
                   pl.BlockSpec((bb, SSM_INNER, SSM_STATE), lambda i: (i, 0, 0))],
        out_shape=[jax.ShapeDtypeStruct((DB * DT, SSM_INNER), BF16),
                   jax.ShapeDtypeStruct((DB, SSM_INNER, SSM_STATE), F32)],
        compiler_params=_cp(("parallel",)),
    )(u, u, u, conv_state, ssm_state, *prm)


def _layer_norm(v, g, b):
    mu = jnp.mean(v, axis=-1, keepdims=True)
    d = v - mu
    var = jnp.mean(d * d, axis=-1, keepdims=True)
    return d * lax.rsqrt(var + LN_EPS) * g + b


def _merge_kernel(x_ref, gp_ref, gs_ref, yap_ref, yas_ref, yhp_ref, yhs_ref, ysp_ref, yss_ref,
                  g1_ref, sh2_ref, sc2_ref, wa_ref, wh_ref, ws_ref, wo_ref, lng_ref, lnb_ref, wr_ref, br_ref,
                  x1_ref, h2_ref, ri_ref, rg_ref, cnt_ref, carry, *, n_prompt_tiles, tm):
    i = pl.program_id(0)

    @pl.when(i == 0)
    def _():
        carry[...] = jnp.zeros_like(carry)

    is_p = i < n_prompt_tiles
    ya = jnp.where(is_p, yap_ref[...], yas_ref[...])
    yh = jnp.where(is_p, yhp_ref[...], yhs_ref[...])
    ys = jnp.where(is_p, ysp_ref[...], yss_ref[...])
    gts = jax.nn.sigmoid(jnp.where(is_p, gp_ref[...], gs_ref[...]).astype(F32))
    m = (gts[:, :D_MODEL] * _dot(ya, wa_ref[...]) + gts[:, D_MODEL:2 * D_MODEL] * _dot(yh, wh_ref[...])
         + gts[:, 2 * D_MODEL:] * _dot(ys, ws_ref[...]))
    mix = _dot(m.astype(BF16), wo_ref[...])
    tg = tm // SUBLANES
    x = x_ref[...]
    v = ALPHA_DN * x + g1_ref[...] * mix.reshape(tg, SUBLANES, D_MODEL)
    x1 = _layer_norm(v, lng_ref[...], lnb_ref[...])
    x1_ref[...] = x1
    h2 = (x1 * (1.0 + sc2_ref[...]) + sh2_ref[...]).reshape(tm, D_MODEL)
    h2_ref[...] = h2
    logits = _dot_f32(h2, wr_ref[...]) + br_ref[...]
    lane = _iota(logits.shape, 1).astype(F32)
    cur = logits
    vals, idxs = [], []
    for _ in range(TOP_K):
        mval = jnp.max(cur, axis=-1, keepdims=True)
        idx = jnp.min(jnp.where(cur == mval, lane, float(N_EXPERTS)), axis=-1, keepdims=True)
        vals.append(mval)
        idxs.append(idx)
        cur = jnp.where(lane == idx, -jnp.inf, cur)
    ex = [jnp.exp(vv - vals[0]) for vv in vals]
    den = ex[0] + ex[1] + ex[2] + ex[3]
    onehot = jnp.zeros(logits.shape, F32)
    for idx in idxs:
        onehot = onehot + (lane == idx).astype(F32)
    strict = (_iota((tm, tm), 0) > _iota((tm, tm), 1)).astype(BF16)
    prefix = _dot(strict, onehot.astype(BF16)) + carry[...]
    carry[...] = carry[...] + jnp.sum(onehot, axis=0, keepdims=True)
    cnt_ref[...] = carry[...]
    l128 = _iota((tm, LANES), 1)
    ri = jnp.zeros((tm, LANES), I32)
    rg = jnp.zeros((tm, LANES), F32)
    for k in range(TOP_K):
        rank = jnp.sum(jnp.where(lane == idxs[k], prefix, 0.0), axis=-1, keepdims=True).astype(I32)
        ri = jnp.where(l128 == k, idxs[k].astype(I32), ri)
        ri = jnp.where(l128 == TOP_K + k, rank, ri)
        rg = jnp.where(l128 == k, ex[k] / den, rg)
    ri_ref[...] = ri
    rg_ref[...] = rg


def _merge_call(x3, u_p, u_s, ya_p, ya_s, yh_p, yh_s, ys_p, ys_s, mod4, wts, layer, n_prompt_rows, tm=256):
    G, _, D = x3.shape
    T = G * SUBLANES
    tg = tm // SUBLANES
    npt = n_prompt_rows // tm
    gcol = COL_GATE // (3 * D)

    def prow(w, c=0):
        return pl.BlockSpec((tm, w), lambda i: (jnp.minimum(i, npt - 1), c))

    def srow(w, c=0):
        return pl.BlockSpec((tm, w), lambda i: (jnp.maximum(i - npt, 0), c))

    def modspec(c):
        return pl.BlockSpec((None, tg, 1, D), lambda i: (layer, i, 0, c))

    in_specs = [pl.BlockSpec((tg, SUBLANES, D), lambda i: (i, 0, 0)),
                prow(3 * D, gcol), srow(3 * D, gcol),
                prow(ATT_OUT), srow(ATT_OUT), prow(HG_WIDTH), srow(HG_WIDTH), prow(SSM_INNER), srow(SSM_INNER),
                modspec(2), modspec(3), modspec(4)] + [_full(a) for a in wts]
    return pl.pallas_call(
        functools.partial(_merge_kernel, n_prompt_tiles=npt, tm=tm),
        grid=(T // tm,),
        in_specs=in_specs,
        out_specs=[pl.BlockSpec((tg, SUBLANES, D), lambda i: (i, 0, 0)),
                   pl.BlockSpec((tm, D), lambda i: (i, 0)),
                   pl.BlockSpec((tm, LANES), lambda i: (i, 0)),
                   pl.BlockSpec((tm, LANES), lambda i: (i, 0)),
                   pl.BlockSpec((1, N_EXPERTS), lambda i: (0, 0))],
        out_shape=[jax.ShapeDtypeStruct((G, SUBLANES, D), F32),
                   jax.ShapeDtypeStruct((T, D), F32),
                   jax.ShapeDtypeStruct((T, LANES), I32),
                   jax.ShapeDtypeStruct((T, LANES), F32),
                   jax.ShapeDtypeStruct((1, N_EXPERTS), F32)],
        scratch_shapes=[pltpu.VMEM((1, N_EXPERTS), F32)],
        compiler_params=_cp(("arbitrary",)),
    )(x3, u_p, u_s, ya_p, ya_s, yh_p, yh_s, ys_p, ys_s, mod4, mod4, mod4, *wts)


def _dispatch_kernel(dest_ref, h_ref, xs_in_ref, xs_ref, sem, *, td):
    del xs_in_ref

    def row_copy(t, k):
        return pltpu.make_async_copy(h_ref.at[pl.ds(t, 1)], xs_ref.at[pl.ds(dest_ref[t * TOP_K + k], 1)], sem)

    def body(t, carry):
        for k in range(TOP_K):
            row_copy(t, k).start()
        return carry

    lax.fori_loop(0, td, body, 0, unroll=DMA_ISSUE_UNROLL)

    def drain(t, carry):
        for k in range(TOP_K):
            row_copy(t, k).wait()
        return carry

    lax.fori_loop(0, td, drain, 0, unroll=DMA_WAIT_UNROLL)


def _dispatch_call(dest_flat, h2, xs_zero, td=256):
    T, D = h2.shape
    return pl.pallas_call(
        functools.partial(_dispatch_kernel, td=td),
        grid=(T // td,),
        in_specs=[pl.BlockSpec((td * TOP_K,), lambda i: (i,), memory_space=pltpu.SMEM),
                  pl.BlockSpec((td, D), lambda i: (i, 0)),
                  pl.BlockSpec(memory_space=pl.ANY)],
        out_specs=pl.BlockSpec(memory_space=pl.ANY),
        out_shape=jax.ShapeDtypeStruct(xs_zero.shape, xs_zero.dtype),
        scratch_shapes=[pltpu.SemaphoreType.DMA(())],
        input_output_aliases={2: 0},
        compiler_params=_cp(("arbitrary",)),
    )(dest_flat, h2, xs_zero)


def _expert_kernel(be_ref, nv_ref, x_ref, wgu_ref, bgu_ref, wdn_ref, bdn_ref, o_ref, wgu_bf, wdn_bf):
    j = pl.program_id(0)
    changed = jnp.logical_or(j == 0, be_ref[j] != be_ref[jnp.maximum(j - 1, 0)])

    @pl.when(changed)
    def _():
        wgu_bf[...] = wgu_ref[...].astype(BF16)
        wdn_bf[...] = wdn_ref[...].astype(BF16)

    @pl.when(j < nv_ref[0])
    def _():
        gu = _dot(x_ref[...].astype(BF16), wgu_bf[...]) + bgu_ref[...]
        g = jnp.minimum(gu[:, :D_FF], SWIGLU_LIMIT)
        up = jnp.clip(gu[:, D_FF:], -SWIGLU_LIMIT, SWIGLU_LIMIT)
        hid = (up + 1.0) * g * jax.nn.sigmoid(SWIGLU_ALPHA * g)
        o_ref[...] = _dot(hid.astype(BF16), wdn_bf[...]) + bdn_ref[...]

    @pl.when(j >= nv_ref[0])
    def _():
        o_ref[...] = jnp.zeros_like(o_ref)


def _expert_call(block_e, nvalid, xs, w_gu, b_gu4, w_dn, b_dn4, layer):
    NS, D = xs.shape
    bm = MOE_BM
    grid_spec = pltpu.PrefetchScalarGridSpec(
        num_scalar_prefetch=2,
        grid=(NS // bm,),
        in_specs=[pl.BlockSpec((bm, D), lambda j, be, nv: (jnp.minimum(j, nv[0] - 1), 0)),
                  pl.BlockSpec((None, None, D, 2 * D_FF), lambda j, be, nv: (layer, be[j], 0, 0)),
                  pl.BlockSpec((None, None, 1, 2 * D_FF), lambda j, be, nv: (layer, be[j], 0, 0)),
                  pl.BlockSpec((None, None, D_FF, D), lambda j, be, nv: (layer, be[j], 0, 0)),
                  pl.BlockSpec((None, None, 1, D), lambda j, be, nv: (layer, be[j], 0, 0))],
        out_specs=pl.BlockSpec((bm, D), lambda j, be, nv: (j, 0)),
        scratch_shapes=[pltpu.VMEM((D, 2 * D_FF), BF16), pltpu.VMEM((D_FF, D), BF16)])
    return pl.pallas_call(
        _expert_kernel,
        grid_spec=grid_spec,
        out_shape=jax.ShapeDtypeStruct((NS, D), F32),
        compiler_params=_cp(("arbitrary",)),
    )(block_e, nvalid, xs, w_gu, b_gu4, w_dn, b_dn4)


def _combine_kernel(dest_ref, rg_ref, x1_ref, g2_ref, lng_ref, lnb_ref, outs_ref, x2_ref, buf, sem, *, tc):
    def row_copy(t, k):
        return pltpu.make_async_copy(outs_ref.at[pl.ds(dest_ref[t * TOP_K + k], 1)], buf.at[k, pl.ds(t, 1)], sem)

    def body(t, carry):
        for k in range(TOP_K):
            row_copy(t, k).start()
        return carry

    lax.fori_loop(0, tc, body, 0, unroll=DMA_ISSUE_UNROLL)

    def drain(t, carry):
        for k in range(TOP_K):
            row_copy(t, k).wait()
        return carry

    lax.fori_loop(0, tc, drain, 0, unroll=DMA_WAIT_UNROLL)
    rg = rg_ref[...]
    ff = rg[:, 0:1] * buf[0]
    for k in range(1, TOP_K):
        ff = ff + rg[:, k:k + 1] * buf[k]
    tg = tc // SUBLANES
    v = ALPHA_DN * x1_ref[...] + g2_ref[...] * ff.reshape(tg, SUBLANES, D_MODEL)
    x2_ref[...] = _layer_norm(v, lng_ref[...], lnb_ref[...])


def _combine_call(dest_flat, rg, x1_3, mod4, lng, lnb, outs, layer, tc=256):
    G, _, D = x1_3.shape
    T = G * SUBLANES
    tg = tc // SUBLANES
    return pl.pallas_call(
        functools.partial(_combine_kernel, tc=tc),
        grid=(T // tc,),
        in_specs=[pl.BlockSpec((tc * TOP_K,), lambda i: (i,), memory_space=pltpu.SMEM),
                  pl.BlockSpec((tc, LANES), lambda i: (i, 0)),
                  pl.BlockSpec((tg, SUBLANES, D), lambda i: (i, 0, 0)),
                  pl.BlockSpec((None, tg, 1, D), lambda i: (layer, i, 0, 5)),
                  _full(lng), _full(lnb),
                  pl.BlockSpec(memory_space=pl.ANY)],
        out_specs=pl.BlockSpec((tg, SUBLANES, D), lambda i: (i, 0, 0)),
        out_shape=jax.ShapeDtypeStruct((G, SUBLANES, D), F32),
        scratch_shapes=[pltpu.VMEM((TOP_K, tc, D), F32), pltpu.SemaphoreType.DMA(())],
        compiler_params=_cp(("arbitrary",)),
    )(dest_flat, rg, x1_3, mod4, lng, lnb, outs)


def _moe_plan(ri, counts):
    bm = MOE_BM
    T = ri.shape[0]
    cnt = counts.reshape(N_EXPERTS).astype(I32)
    padded = (cnt + bm - 1) // bm * bm
    pends = jnp.cumsum(padded)
    pstarts = pends - padded
    eidx = ri[:, :TOP_K]
    rank = ri[:, TOP_K:2 * TOP_K]
    dest = (pstarts[eidx] + rank).reshape(T * TOP_K)
    nb = (T * TOP_K) // bm + N_EXPERTS
    blk = jnp.arange(nb, dtype=I32)
    block_e = jnp.minimum(jnp.sum((pends[None, :] // bm <= blk[:, None]).astype(I32), axis=1), N_EXPERTS - 1)
    nvalid = (pends[-1] // bm).astype(I32).reshape(1)
    last_e = block_e[jnp.maximum(nvalid[0] - 1, 0)]
    block_e = jnp.where(jnp.arange(nb) < nvalid[0], block_e, last_e)
    return dest, block_e, nvalid, nb * bm


def _pack_w_in(w_in):
    depth, D, _ = w_in.shape
    sizes = [ATT_WIDTH] * 3 + [HG_WIDTH] * 4 + [SSM_INNER, CONV_DIM, SSM_HEADS, 3 * D_MODEL]
    pts = [0]
    for s in sizes:
        pts.append(pts[-1] + s)
    att = [w_in[:, :, pts[part] + g * ATT_OUT:pts[part] + (g + 1) * ATT_OUT]
           for g in range(len(DIL_GROUPS)) for part in range(3)]
    att_hg = jnp.concatenate(att + [w_in[:, :, pts[3]:pts[7]]], axis=2)
    z = w_in[:, :, pts[7]:pts[8]]
    xbc = w_in[:, :, pts[8]:pts[9]]
    dt = w_in[:, :, pts[9]:pts[10]]
    gates = w_in[:, :, pts[10]:pts[11]]

    def zeros(n):
        return jnp.zeros((depth, D, n), w_in.dtype)

    packed = jnp.concatenate([att_hg, dt, zeros(COL_Z - COL_DT - SSM_HEADS), z, zeros(COL_XBC - COL_Z - SSM_INNER),
                              xbc, gates], axis=2)
    assert packed.shape[2] == NP_COLS
    return packed.astype(BF16)


def kernel(x_prompt, x_sample, c_prompt, c_sample, cache_kv_w128, cache_kv_w512, cache_kv_w2048, state_hgrn, state_ssm, state_conv, w_in, w_br_att, w_br_hg, w_br_ssm, w_out, conv_w, conv_b, dt_bias, a_log, d_skip, ssm_norm_g, hg_lb, hg_norm_g, ln1_g, ln1_b, ln2_g, ln2_b, w_ada, b_ada, w_router, b_router, w_gu, b_gu, w_dn, b_dn):
    B, S, D = x_prompt.shape
    DB, DT, _ = x_sample.shape
    depth = w_in.shape[0]
    TP = B * S
    TS = DB * DT
    T = TP + TS
    G = T // SUBLANES
    p = dict(conv_w=conv_w, conv_b=conv_b, dt_bias=dt_bias, a_log=a_log, d_skip=d_skip, ssm_norm_g=ssm_norm_g)

    x2 = jnp.concatenate([x_prompt.reshape(TP, D), x_sample.reshape(TS, D)], axis=0)
    x3 = x2.reshape(G, SUBLANES, D)
    c_groups = jnp.concatenate([jnp.repeat(c_prompt, S // SUBLANES, axis=0),
                                jnp.repeat(c_sample, DT // SUBLANES, axis=0)], axis=0)
    mod4 = _ada_call(c_groups, w_ada, b_ada).reshape(depth, G, 1, 6 * D)

    w_in_p = _pack_w_in(w_in)
    cos_p, sin_p = _rope_tables(jnp.arange(S, dtype=I32))
    cos_s, sin_s = _rope_tables(PAST_LEN + jnp.arange(DT, dtype=I32))

    def cache_t(c):
        return jnp.transpose(c, (0, 1, 3, 4, 5, 2)).reshape(depth, DB, 2, ATT_OUT, c.shape[2])

    kv_views = (cache_t(cache_kv_w128), cache_t(cache_kv_w512), cache_t(cache_kv_w2048))
    ssm_state_v = state_ssm.reshape(depth, DB, SSM_INNER, SSM_STATE)
    b_gu4 = b_gu.reshape(depth, N_EXPERTS, 1, 2 * D_FF)
    b_dn4 = b_dn.reshape(depth, N_EXPERTS, 1, D)

    outs_p = [[] for _ in range(6)]
    outs_s = [[] for _ in range(6)]
    xs = None
    for l in range(depth):
        if l > 0:
            x2 = x3.reshape(T, D)
        u_p = _inproj_p_call(x2, mod4, w_in_p, l, B, S)
        u_s = _inproj_s_call(x3, mod4, w_in_p, l, TP // SUBLANES, TS // SUBLANES)

        ya_p, kvp0, kvp1, kvp2 = _attn_p_call(u_p, cos_p, sin_p, B, S)
        outs_p[0].append(kvp0.reshape(B, -1, 2, ATT_HPG, ATT_HEAD_DIM))
        for g, kv in ((1, kvp1), (2, kvp2)):
            outs_p[g].append(jnp.transpose(kv, (1, 0, 2)).reshape(B, -1, 2, ATT_HPG, ATT_HEAD_DIM))
        ya_s, kv0, kv1, kv2 = _attn_s_call(u_s, kv_views, cos_s, sin_s, l, DB, DT)
        for g, kv in enumerate((kv0, kv1, kv2)):
            outs_s[g].append(kv.reshape(DB, DT, 2, ATT_HPG, ATT_HEAD_DIM))

        yh_p, hg_st_p = _hgrn_p_call(u_p, hg_lb, hg_norm_g, l, B, S)
        yh_s, hg_st_s = _hgrn_s_call(u_s, state_hgrn, hg_lb, hg_norm_g, l, 0, DB, DT)
        outs_p[3].append(hg_st_p)
        outs_s[3].append(hg_st_s)

        prm = _ssd_params(p, l)
        ys_p, ssm_st_p = _ssd_p_call(u_p, prm, B, S)
        ys_s, ssm_st_s = _ssd_s_call(u_s, state_conv, ssm_state_v, prm, l, 0, DB, DT)
        outs_p[4].append(ssm_st_p.reshape(B, SSM_HEADS, SSM_HEAD_DIM, SSM_STATE))
        outs_s[4].append(ssm_st_s.reshape(DB, SSM_HEADS, SSM_HEAD_DIM, SSM_STATE))
        xbc_cols = slice(COL_XBC, COL_XBC + CONV_DIM)
        outs_p[5].append(u_p.reshape(B, S, -1)[:, S - (SSM_CONV - 1):, xbc_cols].astype(F32))
        outs_s[5].append(u_s.reshape(DB, DT, -1)[:, DT - (SSM_CONV - 1):, xbc_cols].astype(F32))

        wts = (w_br_att[l].astype(BF16), w_br_hg[l].astype(BF16), w_br_ssm[l].astype(BF16), w_out[l].astype(BF16),
               ln1_g[l].reshape(1, D), ln1_b[l].reshape(1, D), w_router[l], b_router[l].reshape(1, N_EXPERTS))
        x1_3, h2, ri, rg, counts = _merge_call(x3, u_p, u_s, ya_p, ya_s, yh_p, yh_s, ys_p, ys_s, mod4, wts, l, TP)

        dest, block_e, nvalid, ns = _moe_plan(ri, counts)
        xs = _dispatch_call(dest, h2, jnp.zeros((ns, D), F32) if xs is None else xs)
        eo = _expert_call(block_e, nvalid, xs, w_gu, b_gu4, w_dn, b_dn4, l)
        x3 = _combine_call(dest, rg, x1_3, mod4, ln2_g[l].reshape(1, D), ln2_b[l].reshape(1, D), eo, l)

    xf = x3.reshape(T, D)
    y_prompt = xf[:TP].reshape(B, S, D)
    y_sample = xf[TP:].reshape(DB, DT, D)
    st_p = [jnp.stack(o) for o in outs_p]
    st_s = [jnp.stack(o) for o in outs_s]
    return (y_prompt, y_sample, *st_p, *st_s)
```
